```python
import jax, jax.numpy as jnp
from jax import lax
import numpy as np

D_MODEL = 1024
BATCH = 4
SEQ = 4096
DEPTH = 4
DEC_BATCH = 128
DEC_SEQ = 8
PAST_LEN = 2048
PAGE_SIZE = 128

HEAD_DIM = 64
N_HEADS = D_MODEL // HEAD_DIM
H_FOX = N_HEADS // 2
H_SB = N_HEADS - H_FOX
D_FOX = H_FOX * HEAD_DIM
D_SB = H_SB * HEAD_DIM
D_IN = 3 * D_FOX + H_FOX + 3 * D_SB
N_EXPERTS = 16
N_GROUPS = 4
EXPERTS_PER_GROUP = N_EXPERTS // N_GROUPS
TOP_K = 2
D_EXPERT = D_MODEL // 4
Q_BLOCK = 128
DEEPNORM_ALPHA = (2.0 * DEPTH) ** 0.25
DEEPNORM_BETA = (8.0 * DEPTH) ** -0.25
LN_EPS = 1e-5
RMS_EPS = 1e-6
NEG = -1e30

kernel_name = 'fox_stickbreak_hymba_deepnorm_moe_step'


def layer_norm(x, g, b):
    xf = x.astype(jnp.float32)
    mu = jnp.mean(xf, axis=-1, keepdims=True)
    var = jnp.mean(jnp.square(xf - mu), axis=-1, keepdims=True)
    return ((xf - mu) * lax.rsqrt(var + LN_EPS) * g + b).astype(x.dtype)


def rms_norm(x, g):
    xf = x.astype(jnp.float32)
    return (xf * lax.rsqrt(jnp.mean(jnp.square(xf), axis=-1, keepdims=True) + RMS_EPS) * g).astype(x.dtype)


def sweep_query_blocks(block_fn, per_query, q_pos):
    tq = q_pos.shape[0]
    qb = min(Q_BLOCK, tq)
    nb = -(-tq // qb)
    pad = nb * qb - tq

    def split(a):
        a = jnp.pad(a, [(0, 0), (0, pad)] + [(0, 0)] * (a.ndim - 2))
        return jnp.moveaxis(a.reshape(a.shape[0], nb, qb, *a.shape[2:]), 1, 0)

    pos = jnp.pad(q_pos, (0, pad), mode='edge').reshape(nb, qb)
    out = lax.map(block_fn, (tuple(split(a) for a in per_query), pos))
    out = jnp.moveaxis(out, 0, 1)
    out = out.reshape(out.shape[0], nb * qb, *out.shape[3:])
    return out[:, :tq]


def forgetting_attention(q, k, v, logf):
    tq, tk = q.shape[1], k.shape[1]
    c = lax.cumsum(logf.astype(jnp.float32), axis=1)
    c_keys = jnp.transpose(c, (0, 2, 1))[:, :, None, :]
    c_q = c[:, tk - tq:]
    k_pos = jnp.arange(tk)
    q_pos = jnp.arange(tq) + (tk - tq)
    scale = HEAD_DIM ** -0.5

    def block(args):
        (qb, cqb), pb = args
        s = jnp.einsum('bqhd,bkhd->bhqk', qb, k).astype(jnp.float32) * scale
        s = s + jnp.transpose(cqb, (0, 2, 1))[..., None] - c_keys
        s = jnp.where(k_pos[None, :] <= pb[:, None], s, NEG)
        p = jax.nn.softmax(s, axis=-1)
        return jnp.einsum('bhqk,bkhd->bqhd', p.astype(v.dtype), v)

    return sweep_query_blocks(block, (q, c_q), q_pos)


def stick_breaking_attention(q, k, v):
    tq, tk = q.shape[1], k.shape[1]
    k_pos = jnp.arange(tk)
    q_pos = jnp.arange(tq) + (tk - tq)
    scale = HEAD_DIM ** -0.5

    def block(args):
        (qb,), pb = args
        z = jnp.einsum('bqhd,bkhd->bhqk', qb, k).astype(jnp.float32) * scale
        mask = k_pos[None, :] < pb[:, None]
        log_fail = jnp.where(mask, jax.nn.log_sigmoid(-z), 0.0)
        suffix = lax.cumsum(log_fail, axis=3, reverse=True) - log_fail
        a = jnp.where(mask, jnp.exp(jax.nn.log_sigmoid(z) + suffix), 0.0)
        return jnp.einsum('bhqk,bkhd->bqhd', a.astype(v.dtype), v)

    return sweep_query_blocks(block, (q,), q_pos)


def head_group_mixer(h, past, w_in, b_f, g_fox, g_sb, w_o):
    b, t, _ = h.shape
    splits = [D_FOX, 2 * D_FOX, 3 * D_FOX, 3 * D_FOX + H_FOX,
              3 * D_FOX + H_FOX + D_SB, 3 * D_FOX + H_FOX + 2 * D_SB]
    proj = jnp.einsum('btd,de->bte', h, w_in)
    q_f, k_f, v_f, f_logit, q_s, k_s, v_s = jnp.split(proj, splits, axis=-1)
    q_f = q_f.reshape(b, t, H_FOX, HEAD_DIM)
    k_f = k_f.reshape(b, t, H_FOX, HEAD_DIM)
    v_f = v_f.reshape(b, t, H_FOX, HEAD_DIM)
    q_s = q_s.reshape(b, t, H_SB, HEAD_DIM)
    k_s = k_s.reshape(b, t, H_SB, HEAD_DIM)
    v_s = v_s.reshape(b, t, H_SB, HEAD_DIM)
    logf = jax.nn.log_sigmoid(f_logit.astype(jnp.float32) + b_f.astype(jnp.float32))
    new_rows = (k_f, v_f, logf, k_s, v_s)
    if past is None:
        kf_all, vf_all, lf_all, ks_all, vs_all = new_rows
    else:
        kf_all, vf_all, lf_all, ks_all, vs_all = (
            jnp.concatenate([p, n.astype(p.dtype)], axis=1) for p, n in zip(past, new_rows))
    o_f = forgetting_attention(q_f, kf_all, vf_all, lf_all).reshape(b, t, D_FOX)
    o_s = stick_breaking_attention(q_s, ks_all, vs_all).reshape(b, t, D_SB)
    o = jnp.concatenate([rms_norm(o_f, g_fox), rms_norm(o_s, g_sb)], axis=-1)
    return jnp.einsum('bte,ed->btd', o, w_o), new_rows


def shared_router_moe(h, w_router, b_router, w_gate, w_up, w_down):
    b, t, d = h.shape
    tok = h.reshape(b * t, d)
    logits = jnp.einsum('nd,de->ne', tok, w_router).astype(jnp.float32) + b_router.astype(jnp.float32)
    grouped = logits.reshape(-1, N_GROUPS, EXPERTS_PER_GROUP)
    group_score = jnp.sum(lax.top_k(grouped, TOP_K)[0], axis=-1)
    g_sel = jnp.argmax(group_score, axis=-1)
    in_group = (jnp.arange(N_EXPERTS) // EXPERTS_PER_GROUP)[None, :] == g_sel[:, None]
    top_val, top_idx = lax.top_k(jnp.where(in_group, logits, NEG), TOP_K)
    gates = jax.nn.softmax(top_val, axis=-1)
    combine = jnp.sum(jax.nn.one_hot(top_idx, N_EXPERTS, dtype=jnp.float32) * gates[..., None], axis=1)
    hg = jnp.einsum('nd,edf->nef', tok, w_gate)
    hu = jnp.einsum('nd,edf->nef', tok, w_up)
    act = jax.nn.silu(hg) * hu * combine[..., None].astype(tok.dtype)
    return jnp.einsum('nef,efd->nd', act, w_down).reshape(b, t, d)


def trunk_layer(x, past, w_in, b_f, g_fox, g_sb, w_o, ln1_g, ln1_b,
                w_router, b_router, w_gate, w_up, w_down, ln2_g, ln2_b):
    mix, new_rows = head_group_mixer(x, past, w_in, b_f, g_fox, g_sb, w_o)
    x = layer_norm(DEEPNORM_ALPHA * x + mix, ln1_g, ln1_b)
    x = layer_norm(DEEPNORM_ALPHA * x + shared_router_moe(x, w_router, b_router, w_gate, w_up, w_down), ln2_g, ln2_b)
    return x, new_rows


def gather_pages(pool_layer_pages):
    s = pool_layer_pages.shape
    return pool_layer_pages.reshape(s[0], s[1] * s[2], *s[3:])


def setup_inputs(seed: int = 0) -> dict:
    key = jax.random.key(seed)
    ks = jax.random.split(key, 24)
    f32 = jnp.float32
    n_pages = PAST_LEN // PAGE_SIZE
    n_used = DEC_BATCH * n_pages
    n_pool = n_used + max(n_used // 4, 1)

    def nrm(k, shape, scale=1.0):
        return jax.random.normal(k, shape, f32) * scale

    x_prompt = nrm(ks[0], (BATCH, SEQ, D_MODEL))
    x_sample = nrm(ks[1], (DEC_BATCH, DEC_SEQ, D_MODEL))
    cache_fox_k = nrm(ks[2], (DEPTH, n_pool, PAGE_SIZE, H_FOX, HEAD_DIM))
    cache_fox_v = nrm(ks[3], (DEPTH, n_pool, PAGE_SIZE, H_FOX, HEAD_DIM), DEEPNORM_BETA)
    cache_fox_logf = jax.nn.log_sigmoid(3.5 + nrm(ks[4], (DEPTH, n_pool, PAGE_SIZE, H_FOX), 1.5))
    cache_sb_k = nrm(ks[5], (DEPTH, n_pool, PAGE_SIZE, H_SB, HEAD_DIM))
    cache_sb_v = nrm(ks[6], (DEPTH, n_pool, PAGE_SIZE, H_SB, HEAD_DIM), DEEPNORM_BETA)
    page_table = jax.random.permutation(ks[7], n_pool)[:n_used].reshape(DEC_BATCH, n_pages).astype(jnp.int32)

    col = jnp.arange(D_IN)
    v_cols = ((col >= 2 * D_FOX) & (col < 3 * D_FOX)) | (col >= 3 * D_FOX + H_FOX + 2 * D_SB)
    w_in = nrm(ks[8], (DEPTH, D_MODEL, D_IN), D_MODEL ** -0.5) * jnp.where(v_cols, DEEPNORM_BETA, 1.0).astype(f32)
    b_f = jax.random.uniform(ks[9], (DEPTH, H_FOX), f32, 1.0, 6.0)
    g_fox = 1.0 + nrm(ks[10], (DEPTH, D_FOX), 0.1)
    g_sb = 1.0 + nrm(ks[11], (DEPTH, D_SB), 0.1)
    w_o = nrm(ks[12], (DEPTH, D_MODEL, D_MODEL), D_MODEL ** -0.5 * DEEPNORM_BETA)
    ln1_g = 1.0 + nrm(ks[13], (DEPTH, D_MODEL), 0.1)
    ln1_b = nrm(ks[14], (DEPTH, D_MODEL), 0.02)
    w_router = nrm(ks[15], (D_MODEL, N_EXPERTS), D_MODEL ** -0.5)
    b_router = nrm(ks[16], (N_EXPERTS,), 0.01)
    w_gate = nrm(ks[17], (DEPTH, N_EXPERTS, D_MODEL, D_EXPERT), D_MODEL ** -0.5)
    w_up = nrm(ks[18], (DEPTH, N_EXPERTS, D_MODEL, D_EXPERT), D_MODEL ** -0.5 * DEEPNORM_BETA)
    w_down = nrm(ks[19], (DEPTH, N_EXPERTS, D_EXPERT, D_MODEL), D_EXPERT ** -0.5 * DEEPNORM_BETA)
    ln2_g = 1.0 + nrm(ks[20], (DEPTH, D_MODEL), 0.1)
    ln2_b = nrm(ks[21], (DEPTH, D_MODEL), 0.02)
    return {'x_prompt': x_prompt, 'x_sample': x_sample,
            'cache_fox_k': cache_fox_k, 'cache_fox_v': cache_fox_v, 'cache_fox_logf': cache_fox_logf,
            'cache_sb_k': cache_sb_k, 'cache_sb_v': cache_sb_v, 'page_table': page_table,
            'w_in': w_in, 'b_f': b_f, 'g_fox': g_fox, 'g_sb': g_sb, 'w_o': w_o,
            'ln1_g': ln1_g, 'ln1_b': ln1_b, 'w_router': w_router, 'b_router': b_router,
            'w_gate': w_gate, 'w_up': w_up, 'w_down': w_down, 'ln2_g': ln2_g, 'ln2_b': ln2_b}


def reference(x_prompt, x_sample, cache_fox_k, cache_fox_v, cache_fox_logf, cache_sb_k, cache_sb_v,
              page_table, w_in, b_f, g_fox, g_sb, w_o, ln1_g, ln1_b, w_router, b_router,
              w_gate, w_up, w_down, ln2_g, ln2_b):
    y_prompt = x_prompt
    y_sample = x_sample
    rows_p = [[] for _ in range(5)]
    rows_s = [[] for _ in range(5)]
    for l in range(DEPTH):
        weights = (w_in[l], b_f[l], g_fox[l], g_sb[l], w_o[l], ln1_g[l], ln1_b[l],
                   w_router, b_router, w_gate[l], w_up[l], w_down[l], ln2_g[l], ln2_b[l])
        y_prompt, new_p = trunk_layer(y_prompt, None, *weights)
        past = (gather_pages(cache_fox_k[l][page_table]), gather_pages(cache_fox_v[l][page_table]),
                gather_pages(cache_fox_logf[l][page_table]), gather_pages(cache_sb_k[l][page_table]),
                gather_pages(cache_sb_v[l][page_table]))
        y_sample, new_s = trunk_layer(y_sample, past, *weights)
        for i in range(5):
            rows_p[i].append(new_p[i])
            rows_s[i].append(new_s[i])
    new_fox_k_prompt = jnp.stack(rows_p[0])
    new_fox_v_prompt = jnp.stack(rows_p[1])
    new_fox_logf_prompt = jnp.stack(rows_p[2])
    new_sb_k_prompt = jnp.stack(rows_p[3])
    new_sb_v_prompt = jnp.stack(rows_p[4])
    new_fox_k_sample = jnp.stack(rows_s[0])
    new_fox_v_sample = jnp.stack(rows_s[1])
    new_fox_logf_sample = jnp.stack(rows_s[2])
    new_sb_k_sample = jnp.stack(rows_s[3])
    new_sb_v_sample = jnp.stack(rows_s[4])
    return (y_prompt, y_sample,
            new_fox_k_prompt, new_fox_v_prompt, new_fox_logf_prompt, new_sb_k_prompt, new_sb_v_prompt,
            new_fox_k_sample, new_fox_v_sample, new_fox_logf_sample, new_sb_k_sample, new_sb_v_sample)
```

```python
import functools

import jax
import jax.numpy as jnp
from jax import lax
from jax.experimental import pallas as pl
from jax.experimental.pallas import tpu as pltpu

HEAD_DIM = 64
N_GROUPS = 4
TOP_K = 2
LN_EPS = 1e-5
RMS_EPS = 1e-6
NEG = -1e30

LANES = 128
SUBLANES = 8
HEADS_PER_LANE_TILE = LANES // HEAD_DIM
HEAD_SHIFT = HEAD_DIM.bit_length() - 1
V7X_VMEM_BYTES = 64 * 1024 * 1024
ROW_TILE = 512
Q_TILE = 512
SB_KEY_TILE = 256

F32 = jnp.float32
BF16 = jnp.bfloat16


def _vmem_limit(pipelined_bytes, resident_bytes, temp_bytes):
    need = 2 * pipelined_bytes + resident_bytes + temp_bytes
    return int(min(need + need // 4, V7X_VMEM_BYTES - 8 * 1024 * 1024))


def _dot(a, b):
    return jnp.dot(a, b, preferred_element_type=F32)


def _dot_nt(a, b):
    return lax.dot_general(a, b, (((1,), (1,)), ((), ())), preferred_element_type=F32)


def _split_bf16(x, n):
    pieces = []
    r = x
    for _ in range(n):
        h = r.astype(BF16)
        pieces.append(h)
        r = r - h.astype(F32)
    return pieces


def _log_sigmoid(x):
    return jnp.minimum(x, 0.0) - jnp.log(1.0 + jnp.exp(-jnp.abs(x)))


def _layer_norm(h, g, b):
    mu = jnp.mean(h, axis=-1, keepdims=True)
    d = h - mu
    var = jnp.mean(d * d, axis=-1, keepdims=True)
    return d * lax.rsqrt(var + LN_EPS) * g + b


def _rms_norm(o, g):
    return o * lax.rsqrt(jnp.mean(o * o, axis=-1, keepdims=True) + RMS_EPS) * g


def _inproj_kernel(x_ref, w_ref, wf_ref, bf_ref, *refs, d_mix, n_heads, tiles_per_seq,
                   with_cumsum, emit_q32):
    refs = list(refs)
    qkv_ref, kf_ref, vf_ref, ks_ref, vs_ref, logf_ref = refs[:6]
    refs = refs[6:]
    if with_cumsum:
        ccol_ref, crow_ref = refs[:2]
        refs = refs[2:]
    if emit_q32:
        q32_ref = refs[0]
        refs = refs[1:]
    if with_cumsum:
        carry_ref = refs[0]

    scale = HEAD_DIM ** -0.5
    x = x_ref[...].astype(BF16)
    tm = x.shape[0]
    f32_outs = {1: kf_ref, 2: vf_ref, 4: ks_ref, 5: vs_ref}
    for c in range(6):
        y = _dot(x, w_ref[:, c * d_mix:(c + 1) * d_mix])
        if c in (0, 3):
            y = y * scale
            if emit_q32:
                q32_ref[:, (c // 3) * d_mix:(c // 3 + 1) * d_mix] = y
        else:
            f32_outs[c][...] = y
        qkv_ref[:, c * d_mix:(c + 1) * d_mix] = y.astype(BF16)

    lf = _log_sigmoid(_dot(x, wf_ref[...]) + bf_ref[...])
    logf_ref[...] = lf[:, :n_heads]

    if with_cumsum:
        @pl.when(pl.program_id(0) % tiles_per_seq == 0)
        def _():
            carry_ref[...] = jnp.zeros_like(carry_ref)

        r = lax.broadcasted_iota(jnp.int32, (tm, tm), 0)
        c = lax.broadcasted_iota(jnp.int32, (tm, tm), 1)
        tri = jnp.where(c <= r, 1.0, 0.0).astype(BF16)
        local = None
        for piece in _split_bf16(lf, 3):
            part = _dot(tri, piece)
            local = part if local is None else local + part
        cum = local + carry_ref[...]
        carry_ref[...] = cum[tm - 1:tm, :]
        ccol_ref[...] = cum[:, :n_heads]
        crow_ref[...] = cum.T[:n_heads, :]


def _in_projection(x, w_qkv, w_f, b_f, *, seq_len, with_cumsum, emit_q32):
    n, d_model = x.shape
    d_mix = w_qkv.shape[1] // 6
    n_heads = d_mix // HEAD_DIM
    tm = min(ROW_TILE, n)
    assert n % tm == 0 and seq_len % tm == 0 or not with_cumsum
    grid = (n // tm,)
    row = lambda i: (i, 0)
    const = lambda i: (0, 0)
    out_shape = [jax.ShapeDtypeStruct((n, 6 * d_mix), BF16)]
    out_specs = [pl.BlockSpec((tm, 6 * d_mix), row)]
    for _ in range(4):
        out_shape.append(jax.ShapeDtypeStruct((n, d_mix), F32))
        out_specs.append(pl.BlockSpec((tm, d_mix), row))
    out_shape.append(jax.ShapeDtypeStruct((n, n_heads), F32))
    out_specs.append(pl.BlockSpec((tm, n_heads), row))
    scratch = []
    if with_cumsum:
        out_shape.append(jax.ShapeDtypeStruct((n, n_heads), F32))
        out_specs.append(pl.BlockSpec((tm, n_heads), row))
        out_shape.append(jax.ShapeDtypeStruct((n_heads, n), F32))
        out_specs.append(pl.BlockSpec((n_heads, tm), lambda i: (0, i)))
        scratch.append(pltpu.VMEM((1, LANES), F32))
    if emit_q32:
        out_shape.append(jax.ShapeDtypeStruct((n, 2 * d_mix), F32))
        out_specs.append(pl.BlockSpec((tm, 2 * d_mix), row))
    pipelined = tm * d_model * 4 + tm * 6 * d_mix * 2 + 6 * tm * d_mix * 4 + 3 * tm * LANES * 4
    resident = 2 * (d_model * 6 * d_mix * 2 + d_model * LANES * 2)
    temps = 8 * tm * d_mix * 4 + 4 * tm * tm * 4
    kern = functools.partial(_inproj_kernel, d_mix=d_mix, n_heads=n_heads,
                             tiles_per_seq=max(seq_len // tm, 1),
                             with_cumsum=with_cumsum, emit_q32=emit_q32)
    return pl.pallas_call(
        kern,
        out_shape=out_shape,
        grid=grid,
        in_specs=[pl.BlockSpec((tm, d_model), row),
                  pl.BlockSpec(w_qkv.shape, const),
                  pl.BlockSpec(w_f.shape, const),
                  pl.BlockSpec(b_f.shape, const)],
        out_specs=out_specs,
        scratch_shapes=scratch,
        compiler_params=pltpu.CompilerParams(
            dimension_semantics=("arbitrary",),
            vmem_limit_bytes=_vmem_limit(pipelined, resident, temps)),
        name="in_projection",
    )(x, w_qkv, w_f, b_f)


def _pair_masks():
    lane = lax.broadcasted_iota(jnp.int32, (1, LANES), 1)
    return [lane < HEAD_DIM, lane >= HEAD_DIM]


def _fox_prompt_kernel(q_ref, k_ref, v_ref, ccol_ref, crow0_ref, crow1_ref, o_ref,
                       m_scr, l_scr, acc_scr, *, tq):
    hp = pl.program_id(1)
    qi = pl.program_id(2)
    in_head = _pair_masks()
    q = q_ref[...]
    qh = [jnp.where(in_head[hh], q, jnp.zeros_like(q)) for hh in range(2)]
    ccol = ccol_ref[...]
    head_lane = lax.broadcasted_iota(jnp.int32, (1, ccol.shape[1]), 1)
    cq = [jnp.sum(jnp.where(head_lane == HEADS_PER_LANE_TILE * hp + hh, ccol, 0.0),
                  axis=1, keepdims=True) for hh in range(2)]
    crow = [crow0_ref, crow1_ref]

    m_scr[...] = jnp.full_like(m_scr, NEG)
    l_scr[...] = jnp.zeros_like(l_scr)
    acc_scr[...] = jnp.zeros_like(acc_scr)

    def tile(kt, masked):
        k0 = pl.multiple_of(kt * tq, tq)
        k_t = k_ref[pl.ds(k0, tq), :]
        v_t = v_ref[pl.ds(k0, tq), :]
        for hh in range(2):
            s = _dot_nt(qh[hh], k_t) + (cq[hh] - crow[hh][pl.ds(kt, 1), :])
            if masked:
                r = lax.broadcasted_iota(jnp.int32, s.shape, 0)
                c = lax.broadcasted_iota(jnp.int32, s.shape, 1)
                s = jnp.where(c <= r, s, NEG)
            m_prev = m_scr[hh]
            m_new = jnp.maximum(m_prev, jnp.max(s, axis=1, keepdims=True))
            alpha = jnp.exp(m_prev - m_new)
            p = jnp.exp(s - m_new)
            l_scr[hh] = alpha * l_scr[hh] + jnp.sum(p, axis=1, keepdims=True)
            m_scr[hh] = m_new
            acc_scr[hh] = alpha * acc_scr[hh] + _dot(p.astype(BF16), v_t)

    def body(kt, carry):
        tile(kt, False)
        return carry

    lax.fori_loop(0, qi, body, 0)
    tile(qi, True)
    o_ref[...] = jnp.where(in_head[0], acc_scr[0] / l_scr[0], acc_scr[1] / l_scr[1])


def _fox_prompt(qkv, ccol, crow, *, batch, seq_len):
    n, width = qkv.shape
    d_mix = width // 6
    n_heads = d_mix // HEAD_DIM
    n_pairs = d_mix // LANES
    tq = min(Q_TILE, seq_len)
    n_qt = seq_len // tq
    crow4 = crow.reshape(n_heads, batch, n_qt, tq)
    pipelined = tq * LANES * 2 + 2 * seq_len * LANES * 2 + tq * LANES * 4 * 2 + 2 * n_qt * tq * 4
    temps = 8 * tq * tq * 4
    kern = functools.partial(_fox_prompt_kernel, tq=tq)
    crow_spec = lambda hh: pl.BlockSpec(
        (None, None, n_qt, tq), lambda b, hp, qi: (HEADS_PER_LANE_TILE * hp + hh, b, 0, 0))
    return pl.pallas_call(
        kern,
        out_shape=jax.ShapeDtypeStruct((n, d_mix), F32),
        grid=(batch, n_pairs, n_qt),
        in_specs=[pl.BlockSpec((tq, LANES), lambda b, hp, qi: (b * n_qt + qi, hp)),
                  pl.BlockSpec((seq_len, LANES), lambda b, hp, qi: (b, n_pairs + hp)),
                  pl.BlockSpec((seq_len, LANES), lambda b, hp, qi: (b, 2 * n_pairs + hp)),
                  pl.BlockSpec((tq, n_heads), lambda b, hp, qi: (b * n_qt + qi, 0)),
                  crow_spec(0), crow_spec(1)],
        out_specs=pl.BlockSpec((tq, LANES), lambda b, hp, qi: (b * n_qt + qi, hp)),
        scratch_shapes=[pltpu.VMEM((2, tq, 1), F32), pltpu.VMEM((2, tq, 1), F32),
                        pltpu.VMEM((2, tq, LANES), F32)],
        compiler_params=pltpu.CompilerParams(
            dimension_semantics=("parallel", "parallel", "parallel"),
            vmem_limit_bytes=_vmem_limit(pipelined, 4 * tq * LANES * 4, temps)),
        name="fox_prompt",
    )(qkv, qkv, qkv, ccol, crow4, crow4)


def _suffix_matrix(tk):
    j = lax.broadcasted_iota(jnp.int32, (tk, tk), 0)
    s = lax.broadcasted_iota(jnp.int32, (tk, tk), 1)
    return jnp.where(j > s, 1.0, 0.0).astype(BF16)


def _sb_prompt_kernel(q_ref, k_ref, v_ref, o_ref, carry_scr, acc_scr, *, tq, tk):
    qi = pl.program_id(2)
    in_head = _pair_masks()
    q = q_ref[...]
    qh = [jnp.where(in_head[hh], q, jnp.zeros_like(q)) for hh in range(2)]
    w_suffix = _suffix_matrix(tk)
    per_q = tq // tk

    carry_scr[...] = jnp.zeros_like(carry_scr)
    acc_scr[...] = jnp.zeros_like(acc_scr)

    def tile(kb, mask_offset):
        k0 = pl.multiple_of(kb * tk, tk)
        k_t = k_ref[pl.ds(k0, tk), :]
        v_t = v_ref[pl.ds(k0, tk), :]
        for hh in range(2):
            z = _dot_nt(qh[hh], k_t)
            lf = _log_sigmoid(-z)
            if mask_offset is not None:
                r = lax.broadcasted_iota(jnp.int32, z.shape, 0)
                c = lax.broadcasted_iota(jnp.int32, z.shape, 1)
                valid = c + mask_offset < r
                lf = jnp.where(valid, lf, 0.0)
            hi, lo = _split_bf16(lf, 2)
            suffix = _dot(hi, w_suffix) + _dot(lo, w_suffix)
            carry = carry_scr[hh]
            a = jnp.exp(z + lf + suffix + carry)
            if mask_offset is not None:
                a = jnp.where(valid, a, 0.0)
            acc_scr[hh] = acc_scr[hh] + _dot(a.astype(BF16), v_t)
            carry_scr[hh] = carry + jnp.sum(lf, axis=1, keepdims=True)

    for j in reversed(range(per_q)):
        tile(qi * per_q + j, j * tk)

    def body(i, carry):
        tile(qi * per_q - 1 - i, None)
        return carry

    lax.fori_loop(0, qi * per_q, body, 0)
    o_ref[...] = jnp.where(in_head[0], acc_scr[0], acc_scr[1])


def _sb_prompt(qkv, *, batch, seq_len):
    n, width = qkv.shape
    d_mix = width // 6
    n_pairs = d_mix // LANES
    tq = min(Q_TILE, seq_len)
    tk = min(SB_KEY_TILE, tq)
    n_qt = seq_len // tq
    pipelined = tq * LANES * 2 + 2 * seq_len * LANES * 2 + tq * LANES * 4
    temps = 12 * tq * tk * 4
    kern = functools.partial(_sb_prompt_kernel, tq=tq, tk=tk)
    return pl.pallas_call(
        kern,
        out_shape=jax.ShapeDtypeStruct((n, d_mix), F32),
        grid=(batch, n_pairs, n_qt),
        in_specs=[pl.BlockSpec((tq, LANES), lambda b, hp, qi: (b * n_qt + qi, 3 * n_pairs + hp)),
                  pl.BlockSpec((seq_len, LANES), lambda b, hp, qi: (b, 4 * n_pairs + hp)),
                  pl.BlockSpec((seq_len, LANES), lambda b, hp, qi: (b, 5 * n_pairs + hp))],
        out_specs=pl.BlockSpec((tq, LANES), lambda b, hp, qi: (b * n_qt + qi, hp)),
        scratch_shapes=[pltpu.VMEM((2, tq, 1), F32), pltpu.VMEM((2, tq, LANES), F32)],
        compiler_params=pltpu.CompilerParams(
            dimension_semantics=("parallel", "parallel", "parallel"),
            vmem_limit_bytes=_vmem_limit(pipelined, 3 * tq * LANES * 4, temps)),
        name="sb_prompt",
    )(qkv, qkv, qkv)


def _block_diag_queries(qn, n_heads):
    lane_head = lax.broadcasted_iota(jnp.int32, (1, qn.shape[1]), 1) >> HEAD_SHIFT
    rows = [jnp.where(lane_head == h, qn, 0.0) for h in range(n_heads)]
    return jnp.concatenate(rows, axis=0).astype(BF16), lane_head


def _take_head_blocks(o, lane_head, n_heads, dec_seq):
    out = None
    for h in range(n_heads):
        part = jnp.where(lane_head == h, o[h * dec_seq:(h + 1) * dec_seq, :], 0.0)
        out = part if out is None else out + part
    return out


def _pad_rows(x, rows):
    return jnp.concatenate([x, jnp.zeros((rows - x.shape[0], x.shape[1]), x.dtype)], axis=0)


def _sample_scores(qbd, kt_refs, knew_pad):
    parts = [_dot(qbd, ref[...].astype(BF16)) for ref in kt_refs]
    parts.append(_dot_nt(qbd, knew_pad))
    return parts


def _sample_values(p_parts, vt_refs, vnew_pad):
    o = _dot(p_parts[-1], vnew_pad)
    for p, ref in zip(p_parts[:-1], vt_refs):
        o = o + _dot_nt(p, ref[...].astype(BF16))
    return o


def _fox_sample_kernel(pt_ref, q_ref, knew_ref, vnew_ref, lfnew_ref, *refs, n_pages, page, n_heads):
    del pt_ref
    kt_refs = refs[:n_pages]
    vt_refs = refs[n_pages:2 * n_pages]
    lf_refs = refs[2 * n_pages:3 * n_pages]
    o_ref = refs[3 * n_pages]
    dec_seq = q_ref.shape[0]
    qbd, lane_head = _block_diag_queries(q_ref[...], n_heads)
    knew = _pad_rows(knew_ref[...], page).astype(BF16)
    vnew = _pad_rows(vnew_ref[...], page).astype(BF16)
    s_parts = _sample_scores(qbd, kt_refs, knew)

    lf_blocks = [ref[...] for ref in lf_refs] + [lfnew_ref[...]]
    stacked = jnp.concatenate(lf_blocks, axis=0)
    w = jnp.concatenate([_suffix_matrix(page), jnp.ones((page, page), BF16)], axis=1)
    both = None
    for piece in _split_bf16(stacked, 3):
        part = _dot(piece, w)
        both = part if both is None else both + part
    carry = jnp.zeros((n_heads, page), F32)
    r_blocks = [None] * (n_pages + 1)
    for p in reversed(range(n_pages + 1)):
        blk = both[p * n_heads:(p + 1) * n_heads, :]
        r_blocks[p] = blk[:, :page] + carry
        carry = carry + blk[:, page:]

    rows = n_heads * dec_seq
    row_t = lax.broadcasted_iota(jnp.int32, (rows, page), 0) & (dec_seq - 1)
    lane = lax.broadcasted_iota(jnp.int32, (rows, page), 1)
    r_new = jnp.concatenate(
        [jnp.broadcast_to(r_blocks[n_pages][h:h + 1, :], (dec_seq, page)) for h in range(n_heads)], axis=0)
    r_q = jnp.sum(jnp.where(lane == row_t, r_new, 0.0), axis=1, keepdims=True)

    def bias(p):
        blk = r_blocks[p]
        return jnp.concatenate(
            [jnp.broadcast_to(blk[h:h + 1, :], (dec_seq, page)) for h in range(n_heads)], axis=0) - r_q

    s_parts = [s + bias(p) for p, s in enumerate(s_parts)]
    s_parts[n_pages] = jnp.where(lane <= row_t, s_parts[n_pages], NEG)
    m = None
    for s in s_parts:
        cur = jnp.max(s, axis=1, keepdims=True)
        m = cur if m is None else jnp.maximum(m, cur)
    p_parts = [jnp.exp(s - m) for s in s_parts]
    l = None
    for p in p_parts:
        cur = jnp.sum(p, axis=1, keepdims=True)
        l = cur if l is None else l + cur
    o = _sample_values([p.astype(BF16) for p in p_parts], vt_refs, vnew) / l
    o_ref[...] = _take_head_blocks(o, lane_head, n_heads, dec_seq)


def _sb_sample_kernel(pt_ref, q_ref, knew_ref, vnew_ref, *refs, n_pages, page, n_heads):
    del pt_ref
    kt_refs = refs[:n_pages]
    vt_refs = refs[n_pages:2 * n_pages]
    o_ref = refs[2 * n_pages]
    dec_seq = q_ref.shape[0]
    qbd, lane_head = _block_diag_queries(q_ref[...], n_heads)
    knew = _pad_rows(knew_ref[...], page).astype(BF16)
    vnew = _pad_rows(vnew_ref[...], page).astype(BF16)
    z_parts = _sample_scores(qbd, kt_refs, knew)

    rows = n_heads * dec_seq
    row_t = lax.broadcasted_iota(jnp.int32, (rows, page), 0) & (dec_seq - 1)
    lane = lax.broadcasted_iota(jnp.int32, (rows, page), 1)
    valid_new = lane < row_t
    lf_parts = [_log_sigmoid(-z) for z in z_parts]
    lf_parts[n_pages] = jnp.where(valid_new, lf_parts[n_pages], 0.0)

    stacked = jnp.concatenate(lf_parts, axis=0)
    w = jnp.concatenate([_suffix_matrix(page), jnp.ones((page, page), BF16)], axis=1)
    hi, lo = _split_bf16(stacked, 2)
    both = _dot(hi, w) + _dot(lo, w)
    carry = jnp.zeros((rows, page), F32)
    a_parts = [None] * (n_pages + 1)
    for p in reversed(range(n_pages + 1)):
        blk = both[p * rows:(p + 1) * rows, :]
        a = jnp.exp(z_parts[p] + lf_parts[p] + blk[:, :page] + carry)
        if p == n_pages:
            a = jnp.where(valid_new, a, 0.0)
        a_parts[p] = a.astype(BF16)
        carry = carry + blk[:, page:]
    o = _sample_values(a_parts, vt_refs, vnew)
    o_ref[...] = _take_head_blocks(o, lane_head, n_heads, dec_seq)


def _sample_attention(kind, layer, page_table, q32, knew, vnew, cache_kt, cache_vt, lf_new=None, cache_lft=None):
    dec_batch, dec_seq, d2 = q32.shape
    d_mix = d2 // 2
    n_heads = d_mix // HEAD_DIM
    n_pages = page_table.shape[1]
    page = cache_kt.shape[-1]
    col = 0 if kind == "fox" else 1

    def page_spec(shape, p):
        return pl.BlockSpec((None, None) + shape, lambda b, pt: (layer, pt[b, p], 0, 0))

    in_specs = [pl.BlockSpec((None, dec_seq, d_mix), lambda b, pt: (b, 0, col)),
                pl.BlockSpec((None, dec_seq, d_mix), lambda b, pt: (b, 0, 0)),
                pl.BlockSpec((None, dec_seq, d_mix), lambda b, pt: (b, 0, 0))]
    args = [q32, knew, vnew]
    if kind == "fox":
        in_specs.append(pl.BlockSpec((None, n_heads, page), lambda b, pt: (b, 0, 0)))
        args.append(lf_new)
    in_specs += [page_spec((d_mix, page), p) for p in range(n_pages)]
    args += [cache_kt] * n_pages
    in_specs += [page_spec((d_mix, page), p) for p in range(n_pages)]
    args += [cache_vt] * n_pages
    if kind == "fox":
        in_specs += [page_spec((n_heads, page), p) for p in range(n_pages)]
        args += [cache_lft] * n_pages
        kern = functools.partial(_fox_sample_kernel, n_pages=n_pages, page=page, n_heads=n_heads)
    else:
        kern = functools.partial(_sb_sample_kernel, n_pages=n_pages, page=page, n_heads=n_heads)
    pipelined = 2 * n_pages * d_mix * page * 4 + n_pages * SUBLANES * page * 4 + 5 * dec_seq * d_mix * 4
    temps = (2 * n_pages * d_mix * page * 2) // 4 + 24 * (n_pages + 1) * n_heads * dec_seq * page * 4
    return pl.pallas_call(
        kern,
        out_shape=jax.ShapeDtypeStruct((dec_batch, dec_seq, d_mix), F32),
        grid_spec=pltpu.PrefetchScalarGridSpec(
            num_scalar_prefetch=1,
            grid=(dec_batch,),
            in_specs=in_specs,
            out_specs=pl.BlockSpec((None, dec_seq, d_mix), lambda b, pt: (b, 0, 0))),
        compiler_params=pltpu.CompilerParams(
            dimension_semantics=("parallel",),
            vmem_limit_bytes=_vmem_limit(pipelined, 0, temps)),
        name=kind + "_sample",
    )(page_table, *args)


def _outproj_kernel(of_ref, os_ref, x_ref, gf_ref, gs_ref, wo_ref, g_ref, b_ref, y_ref, *, alpha):
    d_fox = of_ref.shape[1]
    of = _rms_norm(of_ref[...], gf_ref[...]).astype(BF16)
    os_ = _rms_norm(os_ref[...], gs_ref[...]).astype(BF16)
    mix = _dot(of, wo_ref[:d_fox, :]) + _dot(os_, wo_ref[d_fox:, :])
    y_ref[...] = _layer_norm(alpha * x_ref[...] + mix, g_ref[...], b_ref[...])


def _out_projection(o_f, o_s, x, g_fox, g_sb, w_o, ln_g, ln_b, *, alpha):
    n, d_model = x.shape
    d_mix = o_f.shape[1]
    tm = min(ROW_TILE, n)
    row = lambda i: (i, 0)
    const = lambda i: (0, 0)
    pipelined = 2 * tm * d_mix * 4 + 2 * tm * d_model * 4
    resident = 2 * (w_o.size * 2 + 4 * d_model * 4)
    temps = 6 * tm * d_model * 4
    return pl.pallas_call(
        functools.partial(_outproj_kernel, alpha=alpha),
        out_shape=jax.ShapeDtypeStruct((n, d_model), F32),
        grid=(n // tm,),
        in_specs=[pl.BlockSpec((tm, d_mix), row), pl.BlockSpec((tm, d_mix), row),
                  pl.BlockSpec((tm, d_model), row),
                  pl.BlockSpec(g_fox.shape, const), pl.BlockSpec(g_sb.shape, const),
                  pl.BlockSpec(w_o.shape, const),
                  pl.BlockSpec(ln_g.shape, const), pl.BlockSpec(ln_b.shape, const)],
        out_specs=pl.BlockSpec((tm, d_model), row),
        compiler_params=pltpu.CompilerParams(
            dimension_semantics=("parallel",),
            vmem_limit_bytes=_vmem_limit(pipelined, resident, temps)),
        name="out_projection",
    )(o_f, o_s, x, g_fox, g_sb, w_o, ln_g, ln_b)


def _route(logits, n_groups):
    tm, n_experts = logits.shape
    per = n_experts // n_groups
    assert per == 4 and TOP_K == 2
    best = None
    g_sel = None
    for g in range(n_groups):
        a, b, c, d = [logits[:, g * per + j:g * per + j + 1] for j in range(per)]
        hi1, lo1 = jnp.maximum(a, b), jnp.minimum(a, b)
        hi2, lo2 = jnp.maximum(c, d), jnp.minimum(c, d)
        score = jnp.maximum(hi1, hi2) + jnp.maximum(jnp.minimum(hi1, hi2), jnp.maximum(lo1, lo2))
        if g == 0:
            best, g_sel = score, jnp.zeros(score.shape, jnp.int32)
        else:
            better = score > best
            best = jnp.where(better, score, best)
            g_sel = jnp.where(better, g, g_sel)
    eidx = lax.broadcasted_iota(jnp.int32, (tm, n_experts), 1)
    eidx_f = eidx.astype(F32)
    masked = jnp.where(eidx >> (per.bit_length() - 1) == g_sel, logits, NEG)
    v1 = jnp.max(masked, axis=1, keepdims=True)
    i1 = jnp.min(jnp.where(masked == v1, eidx_f, float(n_experts)), axis=1, keepdims=True)
    rest = jnp.where(eidx_f == i1, -jnp.inf, masked)
    v2 = jnp.max(rest, axis=1, keepdims=True)
    i2 = jnp.min(jnp.where(rest == v2, eidx_f, float(n_experts)), axis=1, keepdims=True)
    e2 = jnp.exp(v2 - v1)
    den = 1.0 + e2
    return jnp.where(eidx_f == i1, 1.0 / den, 0.0) + jnp.where(eidx_f == i2, e2 / den, 0.0)


def _moe_kernel(x_ref, wr_ref, br_ref, wg_ref, wu_ref, wd_ref, g_ref, b_ref, y_ref, act_scr, *,
                alpha, n_groups):
    x = x_ref[...]
    logits = jnp.dot(x, wr_ref[...], preferred_element_type=F32,
                     precision=lax.Precision.HIGHEST) + br_ref[...]
    combine = _route(logits, n_groups)
    xb = x.astype(BF16)
    n_experts, _, d_expert = wg_ref.shape
    for e in range(n_experts):
        hg = _dot(xb, wg_ref[e])
        hu = _dot(xb, wu_ref[e])
        act = hg * (1.0 / (1.0 + jnp.exp(-hg))) * hu * combine[:, e:e + 1]
        act_scr[:, e * d_expert:(e + 1) * d_expert] = act.astype(BF16)
    moe = _dot(act_scr[...], wd_ref[...])
    y_ref[...] = _layer_norm(alpha * x + moe, g_ref[...], b_ref[...])


def _moe(x, w_router, b_router, w_gate, w_up, w_down, ln_g, ln_b, *, alpha):
    n, d_model = x.shape
    n_experts, _, d_expert = w_gate.shape
    tm = min(ROW_TILE, n)
    row = lambda i: (i, 0)
    const2 = lambda i: (0, 0)
    const3 = lambda i: (0, 0, 0)
    once = pl.Buffered(1)
    pipelined = 2 * tm * d_model * 4
    resident = 3 * w_gate.size * 2 + tm * n_experts * d_expert * 2 + 2 * d_model * LANES * 4
    temps = 6 * tm * d_model * 4 + 6 * tm * d_expert * 4
    return pl.pallas_call(
        functools.partial(_moe_kernel, alpha=alpha, n_groups=N_GROUPS),
        out_shape=jax.ShapeDtypeStruct((n, d_model), F32),
        grid=(n // tm,),
        in_specs=[pl.BlockSpec((tm, d_model), row),
                  pl.BlockSpec(w_router.shape, const2), pl.BlockSpec(b_router.shape, const2),
                  pl.BlockSpec(w_gate.shape, const3, pipeline_mode=once),
                  pl.BlockSpec(w_up.shape, const3, pipeline_mode=once),
                  pl.BlockSpec(w_down.shape, const2, pipeline_mode=once),
                  pl.BlockSpec(ln_g.shape, const2), pl.BlockSpec(ln_b.shape, const2)],
        out_specs=pl.BlockSpec((tm, d_model), row),
        scratch_shapes=[pltpu.VMEM((tm, n_experts * d_expert), BF16)],
        compiler_params=pltpu.CompilerParams(
            dimension_semantics=("parallel",),
            vmem_limit_bytes=_vmem_limit(pipelined, resident, temps)),
        name="moe",
    )(x, w_router, b_router, w_gate, w_up, w_down, ln_g, ln_b)


def kernel(x_prompt, x_sample, cache_fox_k, cache_fox_v, cache_fox_logf, cache_sb_k, cache_sb_v, page_table, w_in, b_f, g_fox, g_sb, w_o, ln1_g, ln1_b, w_router, b_router, w_gate, w_up, w_down, ln2_g, ln2_b):
    batch, seq_len, d_model = x_prompt.shape
    dec_batch, dec_seq, _ = x_sample.shape
    depth, n_pool, page, h_fox, _ = cache_fox_k.shape
    h_sb = cache_sb_k.shape[3]
    assert h_fox == h_sb and dec_seq == SUBLANES and page == LANES
    d_mix = h_fox * HEAD_DIM
    n_experts, d_expert = w_gate.shape[1], w_gate.shape[3]
    alpha = (2.0 * depth) ** 0.25

    f0 = 3 * d_mix
    w_qkv = jnp.concatenate([w_in[:, :, :f0], w_in[:, :, f0 + h_fox:]], axis=-1).astype(BF16)
    w_f = jnp.pad(w_in[:, :, f0:f0 + h_fox], ((0, 0), (0, 0), (0, LANES - h_fox))).astype(BF16)
    b_f_pad = jnp.pad(b_f, ((0, 0), (0, LANES - h_fox)))[:, None, :]
    w_o_b = w_o.astype(BF16)
    w_gate_b = w_gate.astype(BF16)
    w_up_b = w_up.astype(BF16)
    w_down_b = w_down.reshape(depth, n_experts * d_expert, d_model).astype(BF16)
    row2 = lambda a: a[:, None, :]
    g_fox2, g_sb2, ln1_g2, ln1_b2, ln2_g2, ln2_b2 = map(row2, (g_fox, g_sb, ln1_g, ln1_b, ln2_g, ln2_b))
    b_router2 = b_router[None, :]

    def feature_major(c):
        return jnp.transpose(c, (0, 1, 3, 4, 2)).reshape(depth, n_pool, d_mix, page)

    fox_kt, fox_vt = feature_major(cache_fox_k), feature_major(cache_fox_v)
    sb_kt, sb_vt = feature_major(cache_sb_k), feature_major(cache_sb_v)
    fox_lft = jnp.transpose(cache_fox_logf, (0, 1, 3, 2))

    xp = x_prompt.reshape(batch * seq_len, d_model)
    xs = x_sample.reshape(dec_batch * dec_seq, d_model)
    rows_p = [[] for _ in range(5)]
    rows_s = [[] for _ in range(5)]
    for l in range(depth):
        qkv, kf, vf, ks, vs, logf, ccol, crow = _in_projection(
            xp, w_qkv[l], w_f[l], b_f_pad[l], seq_len=seq_len, with_cumsum=True, emit_q32=False)
        o_f = _fox_prompt(qkv, ccol, crow, batch=batch, seq_len=seq_len)
        o_s = _sb_prompt(qkv, batch=batch, seq_len=seq_len)
        xp = _out_projection(o_f, o_s, xp, g_fox2[l], g_sb2[l], w_o_b[l], ln1_g2[l], ln1_b2[l], alpha=alpha)
        xp = _moe(xp, w_router, b_router2, w_gate_b[l], w_up_b[l], w_down_b[l], ln2_g2[l], ln2_b2[l], alpha=alpha)
        for i, r in enumerate((kf, vf, logf, ks, vs)):
            rows_p[i].append(r)

        _, kf, vf, ks, vs, logf, q32 = _in_projection(
            xs, w_qkv[l], w_f[l], b_f_pad[l], seq_len=dec_seq, with_cumsum=False, emit_q32=True)
        q32 = q32.reshape(dec_batch, dec_seq, 2 * d_mix)
        per_row = lambda a: a.reshape(dec_batch, dec_seq, d_mix)
        lf_new = jnp.transpose(logf.reshape(dec_batch, dec_seq, h_fox), (0, 2, 1))
        lf_new = jnp.pad(lf_new, ((0, 0), (0, 0), (0, page - dec_seq)))
        o_f = _sample_attention("fox", l, page_table, q32, per_row(kf), per_row(vf), fox_kt, fox_vt,
                                lf_new, fox_lft)
        o_s = _sample_attention("sb", l, page_table, q32, per_row(ks), per_row(vs), sb_kt, sb_vt)
        o_f = o_f.reshape(dec_batch * dec_seq, d_mix)
        o_s = o_s.reshape(dec_batch * dec_seq, d_mix)
        xs = _out_projection(o_f, o_s, xs, g_fox2[l], g_sb2[l], w_o_b[l], ln1_g2[l], ln1_b2[l], alpha=alpha)
        xs = _moe(xs, w_router, b_router2, w_gate_b[l], w_up_b[l], w_down_b[l], ln2_g2[l], ln2_b2[l], alpha=alpha)
        for i, r in enumerate((kf, vf, logf, ks, vs)):
            rows_s[i].append(r)

    def stacked(rows, lead, heads):
        a = jnp.stack(rows)
        tail = (heads, HEAD_DIM) if a.shape[-1] != heads else (heads,)
        return a.reshape((depth,) + lead + tail)

    lead_p, lead_s = (batch, seq_len), (dec_batch, dec_seq)
    return (xp.reshape(batch, seq_len, d_model), xs.reshape(dec_batch, dec_seq, d_model),
            stacked(rows_p[0], lead_p, h_fox), stacked(rows_p[1], lead_p, h_fox),
            stacked(rows_p[2], lead_p, h_fox), stacked(rows_p[3], lead_p, h_sb),
            stacked(rows_p[4], lead_p, h_sb),
            stacked(rows_s[0], lead_s, h_fox), stacked(rows_s[1], lead_s, h_fox),
            stacked(rows_s[2], lead_s, h_fox), stacked(rows_s[3], lead_s, h_sb),
            stacked(rows_s[4], lead_s, h_sb))
```

```python
import functools

import jax
import jax.numpy as jnp
from jax import lax
from jax.experimental import pallas as pl
from jax.experimental.pallas import tpu as pltpu

HEAD_DIM = 64
N_GROUPS = 4
TOP_K = 2
LN_EPS = 1e-5
RMS_EPS = 1e-6
NEG = -1e30

LANES = 128
SUBLANES = 8
HEADS_PER_LANE_TILE = LANES // HEAD_DIM
HEAD_SHIFT = HEAD_DIM.bit_length() - 1
V7X_VMEM_BYTES = 64 * 1024 * 1024
ROW_TILE = 512
Q_TILE = 512
SB_KEY_TILE = 256
ELEMENTWISE_ROWS = 128

F32 = jnp.float32
BF16 = jnp.bfloat16


def _vmem_limit(pipelined_bytes, resident_bytes, temp_bytes):
    need = 2 * pipelined_bytes + resident_bytes + temp_bytes
    return int(min(need + need // 4, V7X_VMEM_BYTES - 8 * 1024 * 1024))


def _dot(a, b):
    return jnp.dot(a, b, preferred_element_type=F32)


def _dot_nt(a, b):
    return lax.dot_general(a, b, (((1,), (1,)), ((), ())), preferred_element_type=F32)


def _split_bf16(x, n):
    pieces = []
    r = x
    for _ in range(n):
        h = r.astype(BF16)
        pieces.append(h)
        r = r - h.astype(F32)
    return pieces


def _log_sigmoid(x):
    return jnp.minimum(x, 0.0) - jnp.log(1.0 + jnp.exp(-jnp.abs(x)))


def _layer_norm(h, g, b):
    mu = jnp.mean(h, axis=-1, keepdims=True)
    d = h - mu
    var = jnp.mean(d * d, axis=-1, keepdims=True)
    return d * lax.rsqrt(var + LN_EPS) * g + b


def _rms_norm(o, g):
    return o * lax.rsqrt(jnp.mean(o * o, axis=-1, keepdims=True) + RMS_EPS) * g


def _bias_placement(n_heads, d_mix):
    import numpy as np
    place_q = np.zeros((3, LANES, d_mix), np.float32)
    place_k = np.zeros((3, LANES, d_mix), np.float32)
    ones = np.zeros((2, d_mix), np.float32)
    for h in range(n_heads):
        base = (h // HEADS_PER_LANE_TILE) * LANES + (HEAD_DIM if h % HEADS_PER_LANE_TILE == 0 else 0)
        for i in range(3):
            place_q[i, h, base + i] = 1.0
            place_k[i, h, base + 3 + i] = -1.0
            ones[0, base + 3 + i] = 1.0
            ones[1, base + i] = 1.0
    return jnp.asarray(place_q, BF16), jnp.asarray(place_k, BF16), jnp.asarray(ones, F32)


def _inproj_kernel(x_ref, w_ref, wf_ref, bf_ref, *refs, d_mix, n_heads, tiles_per_seq,
                   prompt, n_prev):
    refs = list(refs)
    if prompt:
        pq_ref, pk_ref, ones_ref = refs[:3]
        refs = refs[3 + n_prev:]
        qkv_ref, qb_ref, kb_ref, kf_ref, vf_ref, ks_ref, vs_ref, logf_ref, carry_ref = refs
    else:
        q32_ref, kf_ref, vf_ref, ks_ref, vs_ref, logf_ref = refs

    scale = HEAD_DIM ** -0.5
    x = x_ref[...].astype(BF16)
    tm = x.shape[0]
    f32_outs = {1: kf_ref, 2: vf_ref, 4: ks_ref, 5: vs_ref}
    for c in range(6):
        y = _dot(x, w_ref[:, c * d_mix:(c + 1) * d_mix])
        if c in (0, 3):
            y = y * scale
            if not prompt:
                q32_ref[:, (c // 3) * d_mix:(c // 3 + 1) * d_mix] = y
        else:
            f32_outs[c][...] = y.T if prompt else y
        if prompt:
            qkv_ref[:, c * d_mix:(c + 1) * d_mix] = y.astype(BF16)

    lf = _log_sigmoid(_dot(x, wf_ref[...]) + bf_ref[...])
    logf_ref[...] = lf.T[:n_heads, :] if prompt else lf[:, :n_heads]

    if prompt:
        @pl.when(pl.program_id(0) % tiles_per_seq == 0)
        def _():
            carry_ref[...] = jnp.zeros_like(carry_ref)

        r = lax.broadcasted_iota(jnp.int32, (tm, tm), 0)
        c = lax.broadcasted_iota(jnp.int32, (tm, tm), 1)
        tri = jnp.where(c <= r, 1.0, 0.0).astype(BF16)
        local = None
        for piece in _split_bf16(lf, 3):
            part = _dot(tri, piece)
            local = part if local is None else local + part
        cum = local + carry_ref[...]
        carry_ref[...] = cum[tm - 1:tm, :]
        qb = ones_ref[0:1, :]
        kb = ones_ref[1:2, :]
        for i, piece in enumerate(_split_bf16(cum, 3)):
            qb = qb + _dot(piece, pq_ref[i])
            kb = kb + _dot(piece, pk_ref[i])
        qb_ref[...] = qb.astype(BF16)
        kb_ref[...] = kb.astype(BF16)


def _in_projection(x, w_qkv, w_f, b_f, *, seq_len, layer=None, depth=None, stacked=None):
    n, d_model = x.shape
    d_mix = w_qkv.shape[1] // 6
    n_heads = d_mix // HEAD_DIM
    tm = min(ROW_TILE, n)
    prompt = layer is not None
    grid = (n // tm,)
    row = lambda i: (i, 0)
    const = lambda i: (0, 0)
    args = [x, w_qkv, w_f, b_f]
    in_specs = [pl.BlockSpec((tm, d_model), row),
                pl.BlockSpec(w_qkv.shape, const),
                pl.BlockSpec(w_f.shape, const),
                pl.BlockSpec(b_f.shape, const)]
    scratch = []
    aliases = {}
    if prompt:
        assert n % seq_len == 0 and seq_len % tm == 0
        batch, tps = n // seq_len, seq_len // tm
        place_q, place_k, ones = _bias_placement(n_heads, d_mix)
        args += [place_q, place_k, ones]
        in_specs += [pl.BlockSpec(place_q.shape, lambda i: (0, 0, 0)),
                     pl.BlockSpec(place_k.shape, lambda i: (0, 0, 0)),
                     pl.BlockSpec(ones.shape, const)]
        out_shape = [jax.ShapeDtypeStruct((n, 6 * d_mix), BF16),
                     jax.ShapeDtypeStruct((n, d_mix), BF16),
                     jax.ShapeDtypeStruct((n, d_mix), BF16)]
        out_specs = [pl.BlockSpec((tm, 6 * d_mix), row),
                     pl.BlockSpec((tm, d_mix), row), pl.BlockSpec((tm, d_mix), row)]
        slab = lambda i: (layer, i // tps, 0, i % tps)
        for feat in (d_mix, d_mix, d_mix, d_mix, n_heads):
            out_shape.append(jax.ShapeDtypeStruct((depth, batch, feat, seq_len), F32))
            out_specs.append(pl.BlockSpec((None, None, feat, tm), slab))
        if stacked is not None:
            for j, buf in enumerate(stacked):
                aliases[len(args)] = 3 + j
                args.append(buf)
                in_specs.append(pl.BlockSpec(memory_space=pl.ANY))
        scratch.append(pltpu.VMEM((1, LANES), F32))
    else:
        out_shape = [jax.ShapeDtypeStruct((n, 2 * d_mix), F32)]
        out_specs = [pl.BlockSpec((tm, 2 * d_mix), row)]
        for _ in range(4):
            out_shape.append(jax.ShapeDtypeStruct((n, d_mix), F32))
            out_specs.append(pl.BlockSpec((tm, d_mix), row))
        out_shape.append(jax.ShapeDtypeStruct((n, n_heads), F32))
        out_specs.append(pl.BlockSpec((tm, n_heads), row))
    pipelined = tm * d_model * 4 + tm * 8 * d_mix * 2 + 6 * tm * d_mix * 4 + tm * LANES * 4
    resident = 2 * (d_model * 6 * d_mix * 2 + d_model * LANES * 2 + 6 * LANES * d_mix * 2)
    temps = 8 * tm * d_mix * 4 + 4 * tm * tm * 4
    kern = functools.partial(_inproj_kernel, d_mix=d_mix, n_heads=n_heads,
                             tiles_per_seq=max(seq_len // tm, 1), prompt=prompt,
                             n_prev=len(aliases))
    outs = pl.pallas_call(
        kern,
        out_shape=out_shape,
        grid=grid,
        in_specs=in_specs,
        out_specs=out_specs,
        scratch_shapes=scratch,
        input_output_aliases=aliases,
        compiler_params=pltpu.CompilerParams(
            dimension_semantics=("arbitrary",),
            vmem_limit_bytes=_vmem_limit(pipelined, resident, temps)),
        name="in_projection",
    )(*args)
    if prompt:
        return outs[0], outs[1], outs[2], tuple(outs[3:])
    return tuple(outs)


def _pair_masks():
    lane = lax.broadcasted_iota(jnp.int32, (1, LANES), 1)
    return [lane < HEAD_DIM, lane >= HEAD_DIM]


def _lane_chunks(x):
    return [x[:, c * LANES:(c + 1) * LANES] for c in range(x.shape[1] // LANES)]


def _row_chunks(n_rows):
    step = min(ELEMENTWISE_ROWS, n_rows)
    return [slice(r, r + step) for r in range(0, n_rows, step)]


def _fox_prompt_kernel(q_ref, qb_ref, k_ref, kb_ref, v_ref, o_ref, m_scr, acc_scr, *, tq):
    qi = pl.program_id(2)
    in_head = _pair_masks()
    q = q_ref[...]
    qb = qb_ref[...]
    qa = [jnp.where(in_head[hh], q, qb) for hh in range(2)]
    r = lax.broadcasted_iota(jnp.int32, (tq, tq), 0)
    c = lax.broadcasted_iota(jnp.int32, (tq, tq), 1)
    causal = jnp.where(c <= r, 0.0, NEG)

    m_scr[...] = jnp.full_like(m_scr, NEG)
    acc_scr[...] = jnp.zeros_like(acc_scr)

    def tile(kt, masked):
        k0 = pl.multiple_of(kt * tq, tq)
        k_t = k_ref[pl.ds(k0, tq), :]
        kb_t = kb_ref[pl.ds(k0, tq), :]
        v_t = v_ref[pl.ds(k0, tq), :]
        for hh in range(2):
            ka = jnp.where(in_head[hh], k_t, kb_t)
            va = jnp.where(in_head[hh], v_t, jnp.ones_like(v_t))
            s = _dot_nt(qa[hh], ka)
            if masked:
                s = s + causal
            p_rows, alpha_rows = [], []
            for rows in _row_chunks(tq):
                chunks = _lane_chunks(s[rows])
                part = chunks[0]
                for ch in chunks[1:]:
                    part = jnp.maximum(part, ch)
                m_prev = m_scr[hh, rows]
                m_new = jnp.maximum(m_prev, jnp.max(part, axis=1, keepdims=True))
                alpha_rows.append(jnp.exp(m_prev - m_new))
                p_rows.append(jnp.concatenate(
                    [jnp.exp(ch - m_new).astype(BF16) for ch in chunks], axis=1))
                m_scr[hh, rows] = m_new
            p = jnp.concatenate(p_rows, axis=0)
            alpha = jnp.concatenate(alpha_rows, axis=0)
            acc_scr[hh] = alpha * acc_scr[hh] + _dot(p, va)

    def body(i, carry):
        tile(2 * i, False)
        tile(2 * i + 1, False)
        return carry

    lax.fori_loop(0, qi // 2, body, 0)

    @pl.when(qi % 2 == 1)
    def _():
        tile(qi - 1, False)

    tile(qi, True)
    outs = [acc_scr[hh] / pltpu.roll(acc_scr[hh], HEAD_DIM, axis=1) for hh in range(2)]
    o_ref[...] = jnp.where(in_head[0], outs[0], outs[1])


def _fox_prompt(qkv, qb, kb, *, batch, seq_len):
    n, width = qkv.shape
    d_mix = width // 6
    n_pairs = d_mix // LANES
    tq = min(Q_TILE, seq_len)
    n_qt = seq_len // tq
    pipelined = 2 * tq * LANES * 2 + 3 * seq_len * LANES * 2 + tq * LANES * 4
    temps = 8 * tq * tq * 4
    kern = functools.partial(_fox_prompt_kernel, tq=tq)
    q_map = lambda off: (lambda b, hp, qi: (b * n_qt + qi, off + hp))
    kv_map = lambda off: (lambda b, hp, qi: (b, off + hp))
    return pl.pallas_call(
        kern,
        out_shape=jax.ShapeDtypeStruct((n, d_mix), F32),
        grid=(batch, n_pairs, n_qt),
        in_specs=[pl.BlockSpec((tq, LANES), q_map(0)),
                  pl.BlockSpec((tq, LANES), q_map(0)),
                  pl.BlockSpec((seq_len, LANES), kv_map(n_pairs)),
                  pl.BlockSpec((seq_len, LANES), kv_map(0)),
                  pl.BlockSpec((seq_len, LANES), kv_map(2 * n_pairs))],
        out_specs=pl.BlockSpec((tq, LANES), q_map(0)),
        scratch_shapes=[pltpu.VMEM((2, tq, LANES), F32), pltpu.VMEM((2, tq, LANES), F32)],
        compiler_params=pltpu.CompilerParams(
            dimension_semantics=("parallel", "parallel", "parallel"),
            vmem_limit_bytes=_vmem_limit(pipelined, 4 * tq * LANES * 4, temps)),
        name="fox_prompt",
    )(qkv, qb, qkv, kb, qkv)


def _suffix_matrix(tk):
    j = lax.broadcasted_iota(jnp.int32, (tk, tk), 0)
    s = lax.broadcasted_iota(jnp.int32, (tk, tk), 1)
    return jnp.where(j > s, 1.0, 0.0).astype(BF16)


def _sb_prompt_kernel(q_ref, k_ref, v_ref, o_ref, carry_scr, acc_scr, *, tq, tk):
    qi = pl.program_id(2)
    in_head = _pair_masks()
    q = q_ref[...]
    qh = [jnp.where(in_head[hh], q, jnp.zeros_like(q)) for hh in range(2)]
    w_suffix = _suffix_matrix(tk)
    per_q = tq // tk
    row_chunks = _row_chunks(tq)

    carry_scr[...] = jnp.zeros_like(carry_scr)
    acc_scr[...] = jnp.zeros_like(acc_scr)

    def tile(kb, mask_offset):
        k0 = pl.multiple_of(kb * tk, tk)
        k_t = k_ref[pl.ds(k0, tk), :]
        v_t = v_ref[pl.ds(k0, tk), :]
        for hh in range(2):
            z = _dot_nt(qh[hh], k_t)
            sp_rows, g_rows, tot_rows, valid_rows = [], [], [], []
            for rows in row_chunks:
                z_c = z[rows]
                sp = jnp.maximum(z_c, 0.0) + jnp.log(1.0 + jnp.exp(-jnp.abs(z_c)))
                if mask_offset is not None:
                    r = lax.broadcasted_iota(jnp.int32, z_c.shape, 0) + rows.start
                    c = lax.broadcasted_iota(jnp.int32, z_c.shape, 1)
                    valid_rows.append(c + mask_offset < r)
                    sp = jnp.where(valid_rows[-1], sp, 0.0)
                g_rows.append(z_c - sp)
                sp_rows.append(sp.astype(BF16))
                chunks = _lane_chunks(sp)
                total = chunks[0]
                for ch in chunks[1:]:
                    total = total + ch
                tot_rows.append(jnp.sum(total, axis=1, keepdims=True))
            suffix = _dot(jnp.concatenate(sp_rows, axis=0), w_suffix)
            a_rows = []
            for i, rows in enumerate(row_chunks):
                carry = carry_scr[hh, rows]
                e = g_rows[i] - suffix[rows]
                a = jnp.concatenate([jnp.exp(ch + carry) for ch in _lane_chunks(e)], axis=1)
                if mask_offset is not None:
                    a = jnp.where(valid_rows[i], a, 0.0)
                a_rows.append(a.astype(BF16))
                carry_scr[hh, rows] = carry - tot_rows[i]
            acc_scr[hh] = acc_scr[hh] + _dot(jnp.concatenate(a_rows, axis=0), v_t)

    for j in reversed(range(per_q)):
        tile(qi * per_q + j, j * tk)

    def body(i, carry):
        for j in reversed(range(per_q)):
            tile((qi - 1 - i) * per_q + j, None)
        return carry

    lax.fori_loop(0, qi, body, 0)
    o_ref[...] = jnp.where(in_head[0], acc_scr[0], acc_scr[1])


def _sb_prompt(qkv, *, batch, seq_len):
    n, width = qkv.shape
    d_mix = width // 6
    n_pairs = d_mix // LANES
    tq = min(Q_TILE, seq_len)
    tk = min(SB_KEY_TILE, tq)
    n_qt = seq_len // tq
    pipelined = tq * LANES * 2 + 2 * seq_len * LANES * 2 + tq * LANES * 4
    temps = 12 * tq * tk * 4
    kern = functools.partial(_sb_prompt_kernel, tq=tq, tk=tk)
    return pl.pallas_call(
        kern,
        out_shape=jax.ShapeDtypeStruct((n, d_mix), F32),
        grid=(batch, n_pairs, n_qt),
        in_specs=[pl.BlockSpec((tq, LANES), lambda b, hp, qi: (b * n_qt + qi, 3 * n_pairs + hp)),
                  pl.BlockSpec((seq_len, LANES), lambda b, hp, qi: (b, 4 * n_pairs + hp)),
                  pl.BlockSpec((seq_len, LANES), lambda b, hp, qi: (b, 5 * n_pairs + hp))],
        out_specs=pl.BlockSpec((tq, LANES), lambda b, hp, qi: (b * n_qt + qi, hp)),
        scratch_shapes=[pltpu.VMEM((2, tq, LANES), F32), pltpu.VMEM((2, tq, LANES), F32)],
        compiler_params=pltpu.CompilerParams(
            dimension_semantics=("parallel", "parallel", "parallel"),
            vmem_limit_bytes=_vmem_limit(pipelined, 3 * tq * LANES * 4, temps)),
        name="sb_prompt",
    )(qkv, qkv, qkv)


def _block_diag_queries(qn, n_heads):
    lane_head = lax.broadcasted_iota(jnp.int32, (1, qn.shape[1]), 1) >> HEAD_SHIFT
    rows = [jnp.where(lane_head == h, qn, 0.0) for h in range(n_heads)]
    return jnp.concatenate(rows, axis=0).astype(BF16), lane_head


def _take_head_blocks(o, lane_head, n_heads, dec_seq):
    out = None
    for h in range(n_heads):
        part = jnp.where(lane_head == h, o[h * dec_seq:(h + 1) * dec_seq, :], 0.0)
        out = part if out is None else out + part
    return out


def _pad_rows(x, rows):
    return jnp.concatenate([x, jnp.zeros((rows - x.shape[0], x.shape[1]), x.dtype)], axis=0)


def _sample_scores(qbd, kt_refs, knew_pad):
    parts = [_dot(qbd, ref[...].astype(BF16)) for ref in kt_refs]
    parts.append(_dot_nt(qbd, knew_pad))
    return parts


def _sample_values(p_parts, vt_refs, vnew_pad):
    o = _dot(p_parts[-1], vnew_pad)
    for p, ref in zip(p_parts[:-1], vt_refs):
        o = o + _dot_nt(p, ref[...].astype(BF16))
    return o


def _fox_sample_kernel(pt_ref, q_ref, knew_ref, vnew_ref, lfnew_ref, *refs, n_pages, page, n_heads):
    del pt_ref
    kt_refs = refs[:n_pages]
    vt_refs = refs[n_pages:2 * n_pages]
    lf_refs = refs[2 * n_pages:3 * n_pages]
    o_ref = refs[3 * n_pages]
    dec_seq = q_ref.shape[0]
    qbd, lane_head = _block_diag_queries(q_ref[...], n_heads)
    knew = _pad_rows(knew_ref[...], page).astype(BF16)
    vnew = _pad_rows(vnew_ref[...], page).astype(BF16)
    s_parts = _sample_scores(qbd, kt_refs, knew)

    lf_blocks = [ref[...] for ref in lf_refs] + [lfnew_ref[...]]
    stacked = jnp.concatenate(lf_blocks, axis=0)
    w = jnp.concatenate([_suffix_matrix(page), jnp.ones((page, page), BF16)], axis=1)
    both = None
    for piece in _split_bf16(stacked, 3):
        part = _dot(piece, w)
        both = part if both is None else both + part
    carry = jnp.zeros((n_heads, page), F32)
    r_blocks = [None] * (n_pages + 1)
    for p in reversed(range(n_pages + 1)):
        blk = both[p * n_heads:(p + 1) * n_heads, :]
        r_blocks[p] = blk[:, :page] + carry
        carry = carry + blk[:, page:]

    rows = n_heads * dec_seq
    row_t = lax.broadcasted_iota(jnp.int32, (rows, page), 0) & (dec_seq - 1)
    lane = lax.broadcasted_iota(jnp.int32, (rows, page), 1)
    r_new = jnp.concatenate(
        [jnp.broadcast_to(r_blocks[n_pages][h:h + 1, :], (dec_seq, page)) for h in range(n_heads)], axis=0)
    r_q = jnp.sum(jnp.where(lane == row_t, r_new, 0.0), axis=1, keepdims=True)

    def bias(p):
        blk = r_blocks[p]
        return jnp.concatenate(
            [jnp.broadcast_to(blk[h:h + 1, :], (dec_seq, page)) for h in range(n_heads)], axis=0) - r_q

    s_parts = [s + bias(p) for p, s in enumerate(s_parts)]
    s_parts[n_pages] = jnp.where(lane <= row_t, s_parts[n_pages], NEG)
    m = None
    for s in s_parts:
        cur = jnp.max(s, axis=1, keepdims=True)
        m = cur if m is None else jnp.maximum(m, cur)
    p_parts = [jnp.exp(s - m) for s in s_parts]
    l = None
    for p in p_parts:
        cur = jnp.sum(p, axis=1, keepdims=True)
        l = cur if l is None else l + cur
    o = _sample_values([p.astype(BF16) for p in p_parts], vt_refs, vnew) / l
    o_ref[...] = _take_head_blocks(o, lane_head, n_heads, dec_seq)


def _sb_sample_kernel(pt_ref, q_ref, knew_ref, vnew_ref, *refs, n_pages, page, n_heads):
    del pt_ref
    kt_refs = refs[:n_pages]
    vt_refs = refs[n_pages:2 * n_pages]
    o_ref = refs[2 * n_pages]
    dec_seq = q_ref.shape[0]
    qbd, lane_head = _block_diag_queries(q_ref[...], n_heads)
    knew = _pad_rows(knew_ref[...], page).astype(BF16)
    vnew = _pad_rows(vnew_ref[...], page).astype(BF16)
    z_parts = _sample_scores(qbd, kt_refs, knew)

    rows = n_heads * dec_seq
    row_t = lax.broadcasted_iota(jnp.int32, (rows, page), 0) & (dec_seq - 1)
    lane = lax.broadcasted_iota(jnp.int32, (rows, page), 1)
    valid_new = lane < row_t
    lf_parts = [_log_sigmoid(-z) for z in z_parts]
    lf_parts[n_pages] = jnp.where(valid_new, lf_parts[n_pages], 0.0)

    stacked = jnp.concatenate(lf_parts, axis=0)
    w = jnp.concatenate([_suffix_matrix(page), jnp.ones((page, page), BF16)], axis=1)
    hi, lo = _split_bf16(stacked, 2)
    both = _dot(hi, w) + _dot(lo, w)
    carry = jnp.zeros((rows, page), F32)
    a_parts = [None] * (n_pages + 1)
    for p in reversed(range(n_pages + 1)):
        blk = both[p * rows:(p + 1) * rows, :]
        a = jnp.exp(z_parts[p] + lf_parts[p] + blk[:, :page] + carry)
        if p == n_pages:
            a = jnp.where(valid_new, a, 0.0)
        a_parts[p] = a.astype(BF16)
        carry = carry + blk[:, page:]
    o = _sample_values(a_parts, vt_refs, vnew)
    o_ref[...] = _take_head_blocks(o, lane_head, n_heads, dec_seq)


def _sample_attention(kind, layer, page_table, q32, knew, vnew, cache_kt, cache_vt, lf_new=None, cache_lft=None):
    dec_batch, dec_seq, d2 = q32.shape
    d_mix = d2 // 2
    n_heads = d_mix // HEAD_DIM
    n_pages = page_table.shape[1]
    page = cache_kt.shape[-1]
    col = 0 if kind == "fox" else 1

    def page_spec(shape, p):
        return pl.BlockSpec((None, None) + shape, lambda b, pt: (layer, pt[b, p], 0, 0))

    in_specs = [pl.BlockSpec((None, dec_seq, d_mix), lambda b, pt: (b, 0, col)),
                pl.BlockSpec((None, dec_seq, d_mix), lambda b, pt: (b, 0, 0)),
                pl.BlockSpec((None, dec_seq, d_mix), lambda b, pt: (b, 0, 0))]
    args = [q32, knew, vnew]
    if kind == "fox":
        in_specs.append(pl.BlockSpec((None, n_heads, page), lambda b, pt: (b, 0, 0)))
        args.append(lf_new)
    in_specs += [page_spec((d_mix, page), p) for p in range(n_pages)]
    args += [cache_kt] * n_pages
    in_specs += [page_spec((d_mix, page), p) for p in range(n_pages)]
    args += [cache_vt] * n_pages
    if kind == "fox":
        in_specs += [page_spec((n_heads, page), p) for p in range(n_pages)]
        args += [cache_lft] * n_pages
        kern = functools.partial(_fox_sample_kernel, n_pages=n_pages, page=page, n_heads=n_heads)
    else:
        kern = functools.partial(_sb_sample_kernel, n_pages=n_pages, page=page, n_heads=n_heads)
    pipelined = 2 * n_pages * d_mix * page * 4 + n_pages * SUBLANES * page * 4 + 5 * dec_seq * d_mix * 4
    temps = (2 * n_pages * d_mix * page * 2) // 4 + 24 * (n_pages + 1) * n_heads * dec_seq * page * 4
    return pl.pallas_call(
        kern,
        out_shape=jax.ShapeDtypeStruct((dec_batch, dec_seq, d_mix), F32),
        grid_spec=pltpu.PrefetchScalarGridSpec(
            num_scalar_prefetch=1,
            grid=(dec_batch,),
            in_specs=in_specs,
            out_specs=pl.BlockSpec((None, dec_seq, d_mix), lambda b, pt: (b, 0, 0))),
        compiler_params=pltpu.CompilerParams(
            dimension_semantics=("parallel",),
            vmem_limit_bytes=_vmem_limit(pipelined, 0, temps)),
        name=kind + "_sample",
    )(page_table, *args)


def _outproj_kernel(of_ref, os_ref, x_ref, gf_ref, gs_ref, wo_ref, g_ref, b_ref, y_ref, *, alpha):
    d_fox = of_ref.shape[1]
    of = _rms_norm(of_ref[...], gf_ref[...]).astype(BF16)
    os_ = _rms_norm(os_ref[...], gs_ref[...]).astype(BF16)
    mix = _dot(of, wo_ref[:d_fox, :]) + _dot(os_, wo_ref[d_fox:, :])
    y_ref[...] = _layer_norm(alpha * x_ref[...] + mix, g_ref[...], b_ref[...])


def _out_projection(o_f, o_s, x, g_fox, g_sb, w_o, ln_g, ln_b, *, alpha):
    n, d_model = x.shape
    d_mix = o_f.shape[1]
    tm = min(ROW_TILE, n)
    row = lambda i: (i, 0)
    const = lambda i: (0, 0)
    pipelined = 2 * tm * d_mix * 4 + 2 * tm * d_model * 4
    resident = 2 * (w_o.size * 2 + 4 * d_model * 4)
    temps = 6 * tm * d_model * 4
    return pl.pallas_call(
        functools.partial(_outproj_kernel, alpha=alpha),
        out_shape=jax.ShapeDtypeStruct((n, d_model), F32),
        grid=(n // tm,),
        in_specs=[pl.BlockSpec((tm, d_mix), row), pl.BlockSpec((tm, d_mix), row),
                  pl.BlockSpec((tm, d_model), row),
                  pl.BlockSpec(g_fox.shape, const), pl.BlockSpec(g_sb.shape, const),
                  pl.BlockSpec(w_o.shape, const),
                  pl.BlockSpec(ln_g.shape, const), pl.BlockSpec(ln_b.shape, const)],
        out_specs=pl.BlockSpec((tm, d_model), row),
        compiler_params=pltpu.CompilerParams(
            dimension_semantics=("parallel",),
            vmem_limit_bytes=_vmem_limit(pipelined, resident, temps)),
        name="out_projection",
    )(o_f, o_s, x, g_fox, g_sb, w_o, ln_g, ln_b)


def _route(logits, n_groups):
    tm, n_experts = logits.shape
    per = n_experts // n_groups
    assert per == 4 and TOP_K == 2
    best = None
    g_sel = None
    for g in range(n_groups):
        a, b, c, d = [logits[:, g * per + j:g * per + j + 1] for j in range(per)]
        hi1, lo1 = jnp.maximum(a, b), jnp.minimum(a, b)
        hi2, lo2 = jnp.maximum(c, d), jnp.minimum(c, d)
        score = jnp.maximum(hi1, hi2) + jnp.maximum(jnp.minimum(hi1, hi2), jnp.maximum(lo1, lo2))
        if g == 0:
            best, g_sel = score, jnp.zeros(score.shape, jnp.int32)
        else:
            better = score > best
            best = jnp.where(better, score, best)
            g_sel = jnp.where(better, g, g_sel)
    eidx = lax.broadcasted_iota(jnp.int32, (tm, n_experts), 1)
    eidx_f = eidx.astype(F32)
    masked = jnp.where(eidx >> (per.bit_length() - 1) == g_sel, logits, NEG)
    v1 = jnp.max(masked, axis=1, keepdims=True)
    i1 = jnp.min(jnp.where(masked == v1, eidx_f, float(n_experts)), axis=1, keepdims=True)
    rest = jnp.where(eidx_f == i1, -jnp.inf, masked)
    v2 = jnp.max(rest, axis=1, keepdims=True)
    i2 = jnp.min(jnp.where(rest == v2, eidx_f, float(n_experts)), axis=1, keepdims=True)
    e2 = jnp.exp(v2 - v1)
    den = 1.0 + e2
    return jnp.where(eidx_f == i1, 1.0 / den, 0.0) + jnp.where(eidx_f == i2, e2 / den, 0.0)


def _moe_kernel(x_ref, wr_ref, br_ref, wg_ref, wu_ref, wd_ref, g_ref, b_ref, y_ref, act_scr, *,
                alpha, n_groups):
    x = x_ref[...]
    logits = jnp.dot(x, wr_ref[...], preferred_element_type=F32,
                     precision=lax.Precision.HIGHEST) + br_ref[...]
    combine = _route(logits, n_groups)
    xb = x.astype(BF16)
    n_experts, _, d_expert = wg_ref.shape
    for e in range(n_experts):
        hg = _dot(xb, wg_ref[e])
        hu = _dot(xb, wu_ref[e])
        act = hg * (1.0 / (1.0 + jnp.exp(-hg))) * hu * combine[:, e:e + 1]
        act_scr[:, e * d_expert:(e + 1) * d_expert] = act.astype(BF16)
    moe = _dot(act_scr[...], wd_ref[...])
    y_ref[...] = _layer_norm(alpha * x + moe, g_ref[...], b_ref[...])


def _moe(x, w_router, b_router, w_gate, w_up, w_down, ln_g, ln_b, *, alpha):
    n, d_model = x.shape
    n_experts, _, d_expert = w_gate.shape
    tm = min(ROW_TILE, n)
    row = lambda i: (i, 0)
    const2 = lambda i: (0, 0)
    const3 = lambda i: (0, 0, 0)
    once = pl.Buffered(1)
    pipelined = 2 * tm * d_model * 4
    resident = 3 * w_gate.size * 2 + tm * n_experts * d_expert * 2 + 2 * d_model * LANES * 4
    temps = 6 * tm * d_model * 4 + 6 * tm * d_expert * 4
    return pl.pallas_call(
        functools.partial(_moe_kernel, alpha=alpha, n_groups=N_GROUPS),
        out_shape=jax.ShapeDtypeStruct((n, d_model), F32),
        grid=(n // tm,),
        in_specs=[pl.BlockSpec((tm, d_model), row),
                  pl.BlockSpec(w_router.shape, const2), pl.BlockSpec(b_router.shape, const2),
                  pl.BlockSpec(w_gate.shape, const3, pipeline_mode=once),
                  pl.BlockSpec(w_up.shape, const3, pipeline_mode=once),
                  pl.BlockSpec(w_down.shape, const2, pipeline_mode=once),
                  pl.BlockSpec(ln_g.shape, const2), pl.BlockSpec(ln_b.shape, const2)],
        out_specs=pl.BlockSpec((tm, d_model), row),
        scratch_shapes=[pltpu.VMEM((tm, n_experts * d_expert), BF16)],
        compiler_params=pltpu.CompilerParams(
            dimension_semantics=("parallel",),
            vmem_limit_bytes=_vmem_limit(pipelined, resident, temps)),
        name="moe",
    )(x, w_router, b_router, w_gate, w_up, w_down, ln_g, ln_b)


def kernel(x_prompt, x_sample, cache_fox_k, cache_fox_v, cache_fox_logf, cache_sb_k, cache_sb_v, page_table, w_in, b_f, g_fox, g_sb, w_o, ln1_g, ln1_b, w_router, b_router, w_gate, w_up, w_down, ln2_g, ln2_b):
    batch, seq_len, d_model = x_prompt.shape
    dec_batch, dec_seq, _ = x_sample.shape
    depth, n_pool, page, h_fox, _ = cache_fox_k.shape
    h_sb = cache_sb_k.shape[3]
    assert h_fox == h_sb and dec_seq == SUBLANES and page == LANES
    d_mix = h_fox * HEAD_DIM
    n_experts, d_expert = w_gate.shape[1], w_gate.shape[3]
    alpha = (2.0 * depth) ** 0.25

    f0 = 3 * d_mix
    w_qkv = jnp.concatenate([w_in[:, :, :f0], w_in[:, :, f0 + h_fox:]], axis=-1).astype(BF16)
    w_f = jnp.pad(w_in[:, :, f0:f0 + h_fox], ((0, 0), (0, 0), (0, LANES - h_fox))).astype(BF16)
    b_f_pad = jnp.pad(b_f, ((0, 0), (0, LANES - h_fox)))[:, None, :]
    w_o_b = w_o.astype(BF16)
    w_gate_b = w_gate.astype(BF16)
    w_up_b = w_up.astype(BF16)
    w_down_b = w_down.reshape(depth, n_experts * d_expert, d_model).astype(BF16)
    row2 = lambda a: a[:, None, :]
    g_fox2, g_sb2, ln1_g2, ln1_b2, ln2_g2, ln2_b2 = map(row2, (g_fox, g_sb, ln1_g, ln1_b, ln2_g, ln2_b))
    b_router2 = b_router[None, :]

    def feature_major(c):
        return jnp.transpose(c, (0, 1, 3, 4, 2)).reshape(depth, n_pool, d_mix, page)

    fox_kt, fox_vt = feature_major(cache_fox_k), feature_major(cache_fox_v)
    sb_kt, sb_vt = feature_major(cache_sb_k), feature_major(cache_sb_v)
    fox_lft = jnp.transpose(cache_fox_logf, (0, 1, 3, 2))

    xp = x_prompt.reshape(batch * seq_len, d_model)
    xs = x_sample.reshape(dec_batch * dec_seq, d_model)
    stacked_p = None
    rows_s = [[] for _ in range(5)]
    for l in range(depth):
        qkv, qb, kb, stacked_p = _in_projection(
            xp, w_qkv[l], w_f[l], b_f_pad[l], seq_len=seq_len, layer=l, depth=depth, stacked=stacked_p)
        o_f = _fox_prompt(qkv, qb, kb, batch=batch, seq_len=seq_len)
        o_s = _sb_prompt(qkv, batch=batch, seq_len=seq_len)
        xp = _out_projection(o_f, o_s, xp, g_fox2[l], g_sb2[l], w_o_b[l], ln1_g2[l], ln1_b2[l], alpha=alpha)
        xp = _moe(xp, w_router, b_router2, w_gate_b[l], w_up_b[l], w_down_b[l], ln2_g2[l], ln2_b2[l], alpha=alpha)

        q32, kf, vf, ks, vs, logf = _in_projection(
            xs, w_qkv[l], w_f[l], b_f_pad[l], seq_len=dec_seq)
        q32 = q32.reshape(dec_batch, dec_seq, 2 * d_mix)
        per_row = lambda a: a.reshape(dec_batch, dec_seq, d_mix)
        lf_new = jnp.transpose(logf.reshape(dec_batch, dec_seq, h_fox), (0, 2, 1))
        lf_new = jnp.pad(lf_new, ((0, 0), (0, 0), (0, page - dec_seq)))
        o_f = _sample_attention("fox", l, page_table, q32, per_row(kf), per_row(vf), fox_kt, fox_vt,
                                lf_new, fox_lft)
        o_s = _sample_attention("sb", l, page_table, q32, per_row(ks), per_row(vs), sb_kt, sb_vt)
        o_f = o_f.reshape(dec_batch * dec_seq, d_mix)
        o_s = o_s.reshape(dec_batch * dec_seq, d_mix)
        xs = _out_projection(o_f, o_s, xs, g_fox2[l], g_sb2[l], w_o_b[l], ln1_g2[l], ln1_b2[l], alpha=alpha)
        xs = _moe(xs, w_router, b_router2, w_gate_b[l], w_up_b[l], w_down_b[l], ln2_g2[l], ln2_b2[l], alpha=alpha)
        for i, r in enumerate((kf, vf, logf, ks, vs)):
            rows_s[i].append(r)

    def stacked(rows, lead, heads):
        a = jnp.stack(rows)
        tail = (heads, HEAD_DIM) if a.shape[-1] != heads else (heads,)
        return a.reshape((depth,) + lead + tail)

    def token_major(a, heads):
        return jnp.transpose(a.reshape(depth, batch, heads, HEAD_DIM, seq_len), (0, 1, 4, 2, 3))

    kf_p, vf_p, ks_p, vs_p, logf_p = stacked_p
    lead_s = (dec_batch, dec_seq)
    return (xp.reshape(batch, seq_len, d_model), xs.reshape(dec_batch, dec_seq, d_model),
            token_major(kf_p, h_fox), token_major(vf_p, h_fox),
            jnp.transpose(logf_p, (0, 1, 3, 2)), token_major(ks_p, h_sb), token_major(vs_p, h_sb),
            stacked(rows_s[0], lead_s, h_fox), stacked(rows_s[1], lead_s, h_fox),
            stacked(rows_s[2], lead_s, h_fox), stacked(rows_s[3], lead_s, h_sb),
            stacked(rows_s[4], lead_s, h_sb))
```

```python
import functools

import jax
import jax.numpy as jnp
from jax import lax
from jax.experimental import pallas as pl
from jax.experimental.pallas import tpu as pltpu

HEAD_DIM = 64
N_GROUPS = 4
TOP_K = 2
LN_EPS = 1e-5
RMS_EPS = 1e-6
NEG = -1e30
F32_EXP_UNDERFLOW = -110.0

LANES = 128
SUBLANES = 8
HEADS_PER_LANE_TILE = LANES // HEAD_DIM
HEAD_SHIFT = HEAD_DIM.bit_length() - 1
V7X_VMEM_BYTES = 64 * 1024 * 1024
ROW_TILE = 512
Q_TILE = 512
SB_KEY_TILE = 256
ELEMENTWISE_ROWS = 128

F32 = jnp.float32
BF16 = jnp.bfloat16


def _vmem_limit(pipelined_bytes, resident_bytes, temp_bytes):
    need = 2 * pipelined_bytes + resident_bytes + temp_bytes
    return int(min(need + need // 4, V7X_VMEM_BYTES - 8 * 1024 * 1024))


def _dot(a, b):
    return jnp.dot(a, b, preferred_element_type=F32)


def _dot_nt(a, b):
    return lax.dot_general(a, b, (((1,), (1,)), ((), ())), preferred_element_type=F32)


def _split_bf16(x, n):
    pieces = []
    r = x
    for _ in range(n):
        h = r.astype(BF16)
        pieces.append(h)
        r = r - h.astype(F32)
    return pieces


def _log_sigmoid(x):
    return jnp.minimum(x, 0.0) - jnp.log(1.0 + jnp.exp(-jnp.abs(x)))


def _layer_norm(h, g, b):
    mu = jnp.mean(h, axis=-1, keepdims=True)
    d = h - mu
    var = jnp.mean(d * d, axis=-1, keepdims=True)
    return d * lax.rsqrt(var + LN_EPS) * g + b


def _rms_norm(o, g):
    return o * lax.rsqrt(jnp.mean(o * o, axis=-1, keepdims=True) + RMS_EPS) * g


def _bias_placement(n_heads, d_mix):
    import numpy as np
    place_q = np.zeros((3, LANES, d_mix), np.float32)
    place_k = np.zeros((3, LANES, d_mix), np.float32)
    ones = np.zeros((2, d_mix), np.float32)
    for h in range(n_heads):
        base = (h // HEADS_PER_LANE_TILE) * LANES + (HEAD_DIM if h % HEADS_PER_LANE_TILE == 0 else 0)
        for i in range(3):
            place_q[i, h, base + i] = 1.0
            place_k[i, h, base + 3 + i] = -1.0
            ones[0, base + 3 + i] = 1.0
            ones[1, base + i] = 1.0
    return jnp.asarray(place_q, BF16), jnp.asarray(place_k, BF16), jnp.asarray(ones, F32)


def _inproj_kernel(x_ref, w_ref, wf_ref, bf_ref, *refs, d_mix, n_heads, tiles_per_seq,
                   prompt, n_prev):
    refs = list(refs)
    if prompt:
        pq_ref, pk_ref, ones_ref = refs[:3]
        refs = refs[3 + n_prev:]
        qkv_ref, qb_ref, kb_ref, kf_ref, vf_ref, ks_ref, vs_ref, logf_ref, carry_ref = refs
    else:
        q32_ref, kf_ref, vf_ref, ks_ref, vs_ref, logf_ref = refs

    scale = HEAD_DIM ** -0.5
    x = x_ref[...].astype(BF16)
    tm = x.shape[0]
    f32_outs = {1: kf_ref, 2: vf_ref, 4: ks_ref, 5: vs_ref}
    for c in range(6):
        y = _dot(x, w_ref[:, c * d_mix:(c + 1) * d_mix])
        if c in (0, 3):
            y = y * scale
            if not prompt:
                q32_ref[:, (c // 3) * d_mix:(c // 3 + 1) * d_mix] = y
        else:
            f32_outs[c][...] = y.T if prompt else y
        if prompt:
            qkv_ref[:, c * d_mix:(c + 1) * d_mix] = y.astype(BF16)

    lf = _log_sigmoid(_dot(x, wf_ref[...]) + bf_ref[...])
    logf_ref[...] = lf.T[:n_heads, :] if prompt else lf[:, :n_heads]

    if prompt:
        @pl.when(pl.program_id(0) % tiles_per_seq == 0)
        def _():
            carry_ref[...] = jnp.zeros_like(carry_ref)

        r = lax.broadcasted_iota(jnp.int32, (tm, tm), 0)
        c = lax.broadcasted_iota(jnp.int32, (tm, tm), 1)
        tri = jnp.where(c <= r, 1.0, 0.0).astype(BF16)
        local = None
        for piece in _split_bf16(lf, 3):
            part = _dot(tri, piece)
            local = part if local is None else local + part
        cum = local + carry_ref[...]
        carry_ref[...] = cum[tm - 1:tm, :]
        qb = ones_ref[0:1, :]
        kb = ones_ref[1:2, :]
        for i, piece in enumerate(_split_bf16(cum, 3)):
            qb = qb + _dot(piece, pq_ref[i])
            kb = kb + _dot(piece, pk_ref[i])
        qb_ref[...] = qb.astype(BF16)
        kb_ref[...] = kb.astype(BF16)


def _in_projection(x, w_qkv, w_f, b_f, *, seq_len, layer=None, depth=None, stacked=None):
    n, d_model = x.shape
    d_mix = w_qkv.shape[1] // 6
    n_heads = d_mix // HEAD_DIM
    tm = min(ROW_TILE, n)
    prompt = layer is not None
    grid = (n // tm,)
    row = lambda i: (i, 0)
    const = lambda i: (0, 0)
    args = [x, w_qkv, w_f, b_f]
    in_specs = [pl.BlockSpec((tm, d_model), row),
                pl.BlockSpec(w_qkv.shape, const),
                pl.BlockSpec(w_f.shape, const),
                pl.BlockSpec(b_f.shape, const)]
    scratch = []
    aliases = {}
    if prompt:
        assert n % seq_len == 0 and seq_len % tm == 0
        batch, tps = n // seq_len, seq_len // tm
        place_q, place_k, ones = _bias_placement(n_heads, d_mix)
        args += [place_q, place_k, ones]
        in_specs += [pl.BlockSpec(place_q.shape, lambda i: (0, 0, 0)),
                     pl.BlockSpec(place_k.shape, lambda i: (0, 0, 0)),
                     pl.BlockSpec(ones.shape, const)]
        out_shape = [jax.ShapeDtypeStruct((n, 6 * d_mix), BF16),
                     jax.ShapeDtypeStruct((n, d_mix), BF16),
                     jax.ShapeDtypeStruct((n, d_mix), BF16)]
        out_specs = [pl.BlockSpec((tm, 6 * d_mix), row),
                     pl.BlockSpec((tm, d_mix), row), pl.BlockSpec((tm, d_mix), row)]
        slab = lambda i: (layer, i // tps, 0, i % tps)
        for feat in (d_mix, d_mix, d_mix, d_mix, n_heads):
            out_shape.append(jax.ShapeDtypeStruct((depth, batch, feat, seq_len), F32))
            out_specs.append(pl.BlockSpec((None, None, feat, tm), slab))
        if stacked is not None:
            for j, buf in enumerate(stacked):
                aliases[len(args)] = 3 + j
                args.append(buf)
                in_specs.append(pl.BlockSpec(memory_space=pl.ANY))
        scratch.append(pltpu.VMEM((1, LANES), F32))
    else:
        out_shape = [jax.ShapeDtypeStruct((n, 2 * d_mix), F32)]
        out_specs = [pl.BlockSpec((tm, 2 * d_mix), row)]
        for _ in range(4):
            out_shape.append(jax.ShapeDtypeStruct((n, d_mix), F32))
            out_specs.append(pl.BlockSpec((tm, d_mix), row))
        out_shape.append(jax.ShapeDtypeStruct((n, n_heads), F32))
        out_specs.append(pl.BlockSpec((tm, n_heads), row))
    pipelined = tm * d_model * 4 + tm * 8 * d_mix * 2 + 6 * tm * d_mix * 4 + tm * LANES * 4
    resident = 2 * (d_model * 6 * d_mix * 2 + d_model * LANES * 2 + 6 * LANES * d_mix * 2)
    temps = 8 * tm * d_mix * 4 + 4 * tm * tm * 4
    kern = functools.partial(_inproj_kernel, d_mix=d_mix, n_heads=n_heads,
                             tiles_per_seq=max(seq_len // tm, 1), prompt=prompt,
                             n_prev=len(aliases))
    outs = pl.pallas_call(
        kern,
        out_shape=out_shape,
        grid=grid,
        in_specs=in_specs,
        out_specs=out_specs,
        scratch_shapes=scratch,
        input_output_aliases=aliases,
        compiler_params=pltpu.CompilerParams(
            dimension_semantics=("arbitrary",),
            vmem_limit_bytes=_vmem_limit(pipelined, resident, temps)),
        name="in_projection",
    )(*args)
    if prompt:
        return outs[0], outs[1], outs[2], tuple(outs[3:])
    return tuple(outs)


def _pair_masks():
    lane = lax.broadcasted_iota(jnp.int32, (1, LANES), 1)
    return [lane < HEAD_DIM, lane >= HEAD_DIM]


def _lane_chunks(x):
    return [x[:, c * LANES:(c + 1) * LANES] for c in range(x.shape[1] // LANES)]


def _row_chunks(n_rows):
    step = min(ELEMENTWISE_ROWS, n_rows)
    return [slice(r, r + step) for r in range(0, n_rows, step)]


def _fox_prompt_kernel(q_ref, qb_ref, k_ref, kb_ref, v_ref, o_ref, m_scr, acc_scr, *, tq):
    qi = pl.program_id(2)
    in_head = _pair_masks()
    q = q_ref[...]
    qb = qb_ref[...]
    qa = [jnp.where(in_head[hh], q, qb) for hh in range(2)]
    r = lax.broadcasted_iota(jnp.int32, (tq, tq), 0)
    c = lax.broadcasted_iota(jnp.int32, (tq, tq), 1)
    causal = jnp.where(c <= r, 0.0, NEG)

    m_scr[...] = jnp.full_like(m_scr, NEG)
    acc_scr[...] = jnp.zeros_like(acc_scr)

    def tile(kt, masked):
        k0 = pl.multiple_of(kt * tq, tq)
        k_t = k_ref[pl.ds(k0, tq), :]
        kb_t = kb_ref[pl.ds(k0, tq), :]
        v_t = v_ref[pl.ds(k0, tq), :]
        for hh in range(2):
            ka = jnp.where(in_head[hh], k_t, kb_t)
            va = jnp.where(in_head[hh], v_t, jnp.ones_like(v_t))
            s = _dot_nt(qa[hh], ka)
            if masked:
                s = s + causal
            p_rows, alpha_rows = [], []
            for rows in _row_chunks(tq):
                chunks = _lane_chunks(s[rows])
                part = chunks[0]
                for ch in chunks[1:]:
                    part = jnp.maximum(part, ch)
                m_prev = m_scr[hh, rows]
                m_new = jnp.maximum(m_prev, jnp.max(part, axis=1, keepdims=True))
                alpha_rows.append(jnp.exp(m_prev - m_new))
                p_rows.append(jnp.concatenate(
                    [jnp.exp(ch - m_new).astype(BF16) for ch in chunks], axis=1))
                m_scr[hh, rows] = m_new
            p = jnp.concatenate(p_rows, axis=0)
            alpha = jnp.concatenate(alpha_rows, axis=0)
            acc_scr[hh] = alpha * acc_scr[hh] + _dot(p, va)

    def body(i, carry):
        tile(2 * i, False)
        tile(2 * i + 1, False)
        return carry

    lax.fori_loop(0, qi // 2, body, 0)

    @pl.when(qi % 2 == 1)
    def _():
        tile(qi - 1, False)

    tile(qi, True)
    outs = [acc_scr[hh] / pltpu.roll(acc_scr[hh], HEAD_DIM, axis=1) for hh in range(2)]
    o_ref[...] = jnp.where(in_head[0], outs[0], outs[1])


def _fox_prompt(qkv, qb, kb, *, batch, seq_len):
    n, width = qkv.shape
    d_mix = width // 6
    n_pairs = d_mix // LANES
    tq = min(Q_TILE, seq_len)
    n_qt = seq_len // tq
    pipelined = 2 * tq * LANES * 2 + 3 * seq_len * LANES * 2 + tq * LANES * 4
    temps = 8 * tq * tq * 4
    kern = functools.partial(_fox_prompt_kernel, tq=tq)
    q_map = lambda off: (lambda b, hp, qi: (b * n_qt + qi, off + hp))
    kv_map = lambda off: (lambda b, hp, qi: (b, off + hp))
    return pl.pallas_call(
        kern,
        out_shape=jax.ShapeDtypeStruct((n, d_mix), F32),
        grid=(batch, n_pairs, n_qt),
        in_specs=[pl.BlockSpec((tq, LANES), q_map(0)),
                  pl.BlockSpec((tq, LANES), q_map(0)),
                  pl.BlockSpec((seq_len, LANES), kv_map(n_pairs)),
                  pl.BlockSpec((seq_len, LANES), kv_map(0)),
                  pl.BlockSpec((seq_len, LANES), kv_map(2 * n_pairs))],
        out_specs=pl.BlockSpec((tq, LANES), q_map(0)),
        scratch_shapes=[pltpu.VMEM((2, tq, LANES), F32), pltpu.VMEM((2, tq, LANES), F32)],
        compiler_params=pltpu.CompilerParams(
            dimension_semantics=("parallel", "parallel", "parallel"),
            vmem_limit_bytes=_vmem_limit(pipelined, 4 * tq * LANES * 4, temps)),
        name="fox_prompt",
    )(qkv, qb, qkv, kb, qkv)


def _suffix_matrix(tk):
    j = lax.broadcasted_iota(jnp.int32, (tk, tk), 0)
    s = lax.broadcasted_iota(jnp.int32, (tk, tk), 1)
    return jnp.where(j > s, 1.0, 0.0).astype(BF16)


def _sb_prompt_kernel(q_ref, k_ref, v_ref, o_ref, carry_scr, acc_scr, *, tq, tk):
    qi = pl.program_id(2)
    in_head = _pair_masks()
    q = q_ref[...]
    qh = [jnp.where(in_head[hh], q, jnp.zeros_like(q)) for hh in range(2)]
    w_suffix = _suffix_matrix(tk)
    per_q = tq // tk
    row_chunks = _row_chunks(tq)

    carry_scr[...] = jnp.zeros_like(carry_scr)
    acc_scr[...] = jnp.zeros_like(acc_scr)

    def tile(kb, mask_offset):
        k0 = pl.multiple_of(kb * tk, tk)
        k_t = k_ref[pl.ds(k0, tk), :]
        v_t = v_ref[pl.ds(k0, tk), :]
        for hh in range(2):
            z = _dot_nt(qh[hh], k_t)
            sp_rows, g_rows, tot_rows, valid_rows = [], [], [], []
            for rows in row_chunks:
                z_c = z[rows]
                sp = jnp.maximum(z_c, 0.0) + jnp.log(1.0 + jnp.exp(-jnp.abs(z_c)))
                if mask_offset is not None:
                    r = lax.broadcasted_iota(jnp.int32, z_c.shape, 0) + rows.start
                    c = lax.broadcasted_iota(jnp.int32, z_c.shape, 1)
                    valid_rows.append(c + mask_offset < r)
                    sp = jnp.where(valid_rows[-1], sp, 0.0)
                g_rows.append(z_c - sp)
                sp_rows.append(sp.astype(BF16))
                chunks = _lane_chunks(sp)
                total = chunks[0]
                for ch in chunks[1:]:
                    total = total + ch
                tot_rows.append(jnp.sum(total, axis=1, keepdims=True))
            suffix = _dot(jnp.concatenate(sp_rows, axis=0), w_suffix)
            a_rows = []
            for i, rows in enumerate(row_chunks):
                carry = carry_scr[hh, rows]
                e = g_rows[i] - suffix[rows]
                a = jnp.concatenate([jnp.exp(ch + carry) for ch in _lane_chunks(e)], axis=1)
                if mask_offset is not None:
                    a = jnp.where(valid_rows[i], a, 0.0)
                a_rows.append(a.astype(BF16))
                carry_scr[hh, rows] = carry - tot_rows[i]
            acc_scr[hh] = acc_scr[hh] + _dot(jnp.concatenate(a_rows, axis=0), v_t)

    for j in reversed(range(per_q)):
        tile(qi * per_q + j, j * tk)

    def still_weighted():
        worst = jnp.max(jnp.maximum(carry_scr[0], carry_scr[1]))
        return (worst > F32_EXP_UNDERFLOW).astype(jnp.int32)

    def cond(state):
        i, alive = state
        return jnp.logical_and(i < qi, alive > 0)

    def body(state):
        i, _ = state
        for j in reversed(range(per_q)):
            tile((qi - 1 - i) * per_q + j, None)
        return i + 1, still_weighted()

    lax.while_loop(cond, body, (jnp.int32(0), still_weighted()))
    o_ref[...] = jnp.where(in_head[0], acc_scr[0], acc_scr[1])


def _sb_prompt(qkv, *, batch, seq_len):
    n, width = qkv.shape
    d_mix = width // 6
    n_pairs = d_mix // LANES
    tq = min(Q_TILE, seq_len)
    tk = min(SB_KEY_TILE, tq)
    n_qt = seq_len // tq
    pipelined = tq * LANES * 2 + 2 * seq_len * LANES * 2 + tq * LANES * 4
    temps = 12 * tq * tk * 4
    kern = functools.partial(_sb_prompt_kernel, tq=tq, tk=tk)
    return pl.pallas_call(
        kern,
        out_shape=jax.ShapeDtypeStruct((n, d_mix), F32),
        grid=(batch, n_pairs, n_qt),
        in_specs=[pl.BlockSpec((tq, LANES), lambda b, hp, qi: (b * n_qt + qi, 3 * n_pairs + hp)),
                  pl.BlockSpec((seq_len, LANES), lambda b, hp, qi: (b, 4 * n_pairs + hp)),
                  pl.BlockSpec((seq_len, LANES), lambda b, hp, qi: (b, 5 * n_pairs + hp))],
        out_specs=pl.BlockSpec((tq, LANES), lambda b, hp, qi: (b * n_qt + qi, hp)),
        scratch_shapes=[pltpu.VMEM((2, tq, LANES), F32), pltpu.VMEM((2, tq, LANES), F32)],
        compiler_params=pltpu.CompilerParams(
            dimension_semantics=("parallel", "parallel", "parallel"),
            vmem_limit_bytes=_vmem_limit(pipelined, 3 * tq * LANES * 4, temps)),
        name="sb_prompt",
    )(qkv, qkv, qkv)


def _block_diag_queries(qn, n_heads):
    lane_head = lax.broadcasted_iota(jnp.int32, (1, qn.shape[1]), 1) >> HEAD_SHIFT
    rows = [jnp.where(lane_head == h, qn, 0.0) for h in range(n_heads)]
    return jnp.concatenate(rows, axis=0).astype(BF16), lane_head


def _take_head_blocks(o, lane_head, n_heads, dec_seq):
    out = None
    for h in range(n_heads):
        part = jnp.where(lane_head == h, o[h * dec_seq:(h + 1) * dec_seq, :], 0.0)
        out = part if out is None else out + part
    return out


def _pad_rows(x, rows):
    return jnp.concatenate([x, jnp.zeros((rows - x.shape[0], x.shape[1]), x.dtype)], axis=0)


def _sample_scores(qbd, kt_refs, knew_pad):
    parts = [_dot(qbd, ref[...].astype(BF16)) for ref in kt_refs]
    parts.append(_dot_nt(qbd, knew_pad))
    return parts


def _sample_values(p_parts, vt_refs, vnew_pad):
    o = _dot(p_parts[-1], vnew_pad)
    for p, ref in zip(p_parts[:-1], vt_refs):
        o = o + _dot_nt(p, ref[...].astype(BF16))
    return o


def _fox_sample_kernel(pt_ref, q_ref, knew_ref, vnew_ref, lfnew_ref, *refs, n_pages, page, n_heads):
    del pt_ref
    kt_refs = refs[:n_pages]
    vt_refs = refs[n_pages:2 * n_pages]
    lf_refs = refs[2 * n_pages:3 * n_pages]
    o_ref = refs[3 * n_pages]
    dec_seq = q_ref.shape[0]
    qbd, lane_head = _block_diag_queries(q_ref[...], n_heads)
    knew = _pad_rows(knew_ref[...], page).astype(BF16)
    vnew = _pad_rows(vnew_ref[...], page).astype(BF16)
    s_parts = _sample_scores(qbd, kt_refs, knew)

    lf_blocks = [ref[...] for ref in lf_refs] + [lfnew_ref[...]]
    stacked = jnp.concatenate(lf_blocks, axis=0)
    w = jnp.concatenate([_suffix_matrix(page), jnp.ones((page, page), BF16)], axis=1)
    both = None
    for piece in _split_bf16(stacked, 3):
        part = _dot(piece, w)
        both = part if both is None else both + part
    carry = jnp.zeros((n_heads, page), F32)
    r_blocks = [None] * (n_pages + 1)
    for p in reversed(range(n_pages + 1)):
        blk = both[p * n_heads:(p + 1) * n_heads, :]
        r_blocks[p] = blk[:, :page] + carry
        carry = carry + blk[:, page:]

    rows = n_heads * dec_seq
    row_t = lax.broadcasted_iota(jnp.int32, (rows, page), 0) & (dec_seq - 1)
    lane = lax.broadcasted_iota(jnp.int32, (rows, page), 1)
    r_new = jnp.concatenate(
        [jnp.broadcast_to(r_blocks[n_pages][h:h + 1, :], (dec_seq, page)) for h in range(n_heads)], axis=0)
    r_q = jnp.sum(jnp.where(lane == row_t, r_new, 0.0), axis=1, keepdims=True)

    def bias(p):
        blk = r_blocks[p]
        return jnp.concatenate(
            [jnp.broadcast_to(blk[h:h + 1, :], (dec_seq, page)) for h in range(n_heads)], axis=0) - r_q

    s_parts = [s + bias(p) for p, s in enumerate(s_parts)]
    s_parts[n_pages] = jnp.where(lane <= row_t, s_parts[n_pages], NEG)
    m = None
    for s in s_parts:
        cur = jnp.max(s, axis=1, keepdims=True)
        m = cur if m is None else jnp.maximum(m, cur)
    p_parts = [jnp.exp(s - m) for s in s_parts]
    l = None
    for p in p_parts:
        cur = jnp.sum(p, axis=1, keepdims=True)
        l = cur if l is None else l + cur
    o = _sample_values([p.astype(BF16) for p in p_parts], vt_refs, vnew) / l
    o_ref[...] = _take_head_blocks(o, lane_head, n_heads, dec_seq)


def _sb_sample_kernel(pt_ref, q_ref, knew_ref, vnew_ref, *refs, n_pages, page, n_heads):
    del pt_ref
    kt_refs = refs[:n_pages]
    vt_refs = refs[n_pages:2 * n_pages]
    o_ref = refs[2 * n_pages]
    dec_seq = q_ref.shape[0]
    qbd, lane_head = _block_diag_queries(q_ref[...], n_heads)
    knew = _pad_rows(knew_ref[...], page).astype(BF16)
    vnew = _pad_rows(vnew_ref[...], page).astype(BF16)
    z_parts = _sample_scores(qbd, kt_refs, knew)

    rows = n_heads * dec_seq
    row_t = lax.broadcasted_iota(jnp.int32, (rows, page), 0) & (dec_seq - 1)
    lane = lax.broadcasted_iota(jnp.int32, (rows, page), 1)
    valid_new = lane < row_t
    lf_parts = [_log_sigmoid(-z) for z in z_parts]
    lf_parts[n_pages] = jnp.where(valid_new, lf_parts[n_pages], 0.0)

    stacked = jnp.concatenate(lf_parts, axis=0)
    w = jnp.concatenate([_suffix_matrix(page), jnp.ones((page, page), BF16)], axis=1)
    hi, lo = _split_bf16(stacked, 2)
    both = _dot(hi, w) + _dot(lo, w)
    carry = jnp.zeros((rows, page), F32)
    a_parts = [None] * (n_pages + 1)
    for p in reversed(range(n_pages + 1)):
        blk = both[p * rows:(p + 1) * rows, :]
        a = jnp.exp(z_parts[p] + lf_parts[p] + blk[:, :page] + carry)
        if p == n_pages:
            a = jnp.where(valid_new, a, 0.0)
        a_parts[p] = a.astype(BF16)
        carry = carry + blk[:, page:]
    o = _sample_values(a_parts, vt_refs, vnew)
    o_ref[...] = _take_head_blocks(o, lane_head, n_heads, dec_seq)


def _sample_attention(kind, layer, page_table, q32, knew, vnew, cache_kt, cache_vt, lf_new=None, cache_lft=None):
    dec_batch, dec_seq, d2 = q32.shape
    d_mix = d2 // 2
    n_heads = d_mix // HEAD_DIM
    n_pages = page_table.shape[1]
    page = cache_kt.shape[-1]
    col = 0 if kind == "fox" else 1

    def page_spec(shape, p):
        return pl.BlockSpec((None, None) + shape, lambda b, pt: (layer, pt[b, p], 0, 0))

    in_specs = [pl.BlockSpec((None, dec_seq, d_mix), lambda b, pt: (b, 0, col)),
                pl.BlockSpec((None, dec_seq, d_mix), lambda b, pt: (b, 0, 0)),
                pl.BlockSpec((None, dec_seq, d_mix), lambda b, pt: (b, 0, 0))]
    args = [q32, knew, vnew]
    if kind == "fox":
        in_specs.append(pl.BlockSpec((None, n_heads, page), lambda b, pt: (b, 0, 0)))
        args.append(lf_new)
    in_specs += [page_spec((d_mix, page), p) for p in range(n_pages)]
    args += [cache_kt] * n_pages
    in_specs += [page_spec((d_mix, page), p) for p in range(n_pages)]
    args += [cache_vt] * n_pages
    if kind == "fox":
        in_specs += [page_spec((n_heads, page), p) for p in range(n_pages)]
        args += [cache_lft] * n_pages
        kern = functools.partial(_fox_sample_kernel, n_pages=n_pages, page=page, n_heads=n_heads)
    else:
        kern = functools.partial(_sb_sample_kernel, n_pages=n_pages, page=page, n_heads=n_heads)
    pipelined = 2 * n_pages * d_mix * page * 4 + n_pages * SUBLANES * page * 4 + 5 * dec_seq * d_mix * 4
    temps = (2 * n_pages * d_mix * page * 2) // 4 + 24 * (n_pages + 1) * n_heads * dec_seq * page * 4
    return pl.pallas_call(
        kern,
        out_shape=jax.ShapeDtypeStruct((dec_batch, dec_seq, d_mix), F32),
        grid_spec=pltpu.PrefetchScalarGridSpec(
            num_scalar_prefetch=1,
            grid=(dec_batch,),
            in_specs=in_specs,
            out_specs=pl.BlockSpec((None, dec_seq, d_mix), lambda b, pt: (b, 0, 0))),
        compiler_params=pltpu.CompilerParams(
            dimension_semantics=("parallel",),
            vmem_limit_bytes=_vmem_limit(pipelined, 0, temps)),
        name=kind + "_sample",
    )(page_table, *args)


def _outproj_kernel(of_ref, os_ref, x_ref, gf_ref, gs_ref, wo_ref, g_ref, b_ref, y_ref, *, alpha):
    d_fox = of_ref.shape[1]
    of = _rms_norm(of_ref[...], gf_ref[...]).astype(BF16)
    os_ = _rms_norm(os_ref[...], gs_ref[...]).astype(BF16)
    mix = _dot(of, wo_ref[:d_fox, :]) + _dot(os_, wo_ref[d_fox:, :])
    y_ref[...] = _layer_norm(alpha * x_ref[...] + mix, g_ref[...], b_ref[...])


def _out_projection(o_f, o_s, x, g_fox, g_sb, w_o, ln_g, ln_b, *, alpha):
    n, d_model = x.shape
    d_mix = o_f.shape[1]
    tm = min(ROW_TILE, n)
    row = lambda i: (i, 0)
    const = lambda i: (0, 0)
    pipelined = 2 * tm * d_mix * 4 + 2 * tm * d_model * 4
    resident = 2 * (w_o.size * 2 + 4 * d_model * 4)
    temps = 6 * tm * d_model * 4
    return pl.pallas_call(
        functools.partial(_outproj_kernel, alpha=alpha),
        out_shape=jax.ShapeDtypeStruct((n, d_model), F32),
        grid=(n // tm,),
        in_specs=[pl.BlockSpec((tm, d_mix), row), pl.BlockSpec((tm, d_mix), row),
                  pl.BlockSpec((tm, d_model), row),
                  pl.BlockSpec(g_fox.shape, const), pl.BlockSpec(g_sb.shape, const),
                  pl.BlockSpec(w_o.shape, const),
                  pl.BlockSpec(ln_g.shape, const), pl.BlockSpec(ln_b.shape, const)],
        out_specs=pl.BlockSpec((tm, d_model), row),
        compiler_params=pltpu.CompilerParams(
            dimension_semantics=("parallel",),
            vmem_limit_bytes=_vmem_limit(pipelined, resident, temps)),
        name="out_projection",
    )(o_f, o_s, x, g_fox, g_sb, w_o, ln_g, ln_b)


def _route(logits, n_groups):
    tm, n_experts = logits.shape
    per = n_experts // n_groups
    assert per == 4 and TOP_K == 2
    best = None
    g_sel = None
    for g in range(n_groups):
        a, b, c, d = [logits[:, g * per + j:g * per + j + 1] for j in range(per)]
        hi1, lo1 = jnp.maximum(a, b), jnp.minimum(a, b)
        hi2, lo2 = jnp.maximum(c, d), jnp.minimum(c, d)
        score = jnp.maximum(hi1, hi2) + jnp.maximum(jnp.minimum(hi1, hi2), jnp.maximum(lo1, lo2))
        if g == 0:
            best, g_sel = score, jnp.zeros(score.shape, jnp.int32)
        else:
            better = score > best
            best = jnp.where(better, score, best)
            g_sel = jnp.where(better, g, g_sel)
    eidx = lax.broadcasted_iota(jnp.int32, (tm, n_experts), 1)
    eidx_f = eidx.astype(F32)
    masked = jnp.where(eidx >> (per.bit_length() - 1) == g_sel, logits, NEG)
    v1 = jnp.max(masked, axis=1, keepdims=True)
    i1 = jnp.min(jnp.where(masked == v1, eidx_f, float(n_experts)), axis=1, keepdims=True)
    rest = jnp.where(eidx_f == i1, -jnp.inf, masked)
    v2 = jnp.max(rest, axis=1, keepdims=True)
    i2 = jnp.min(jnp.where(rest == v2, eidx_f, float(n_experts)), axis=1, keepdims=True)
    e2 = jnp.exp(v2 - v1)
    den = 1.0 + e2
    return jnp.where(eidx_f == i1, 1.0 / den, 0.0) + jnp.where(eidx_f == i2, e2 / den, 0.0)


def _moe_kernel(x_ref, wr_ref, br_ref, wg_ref, wu_ref, wd_ref, g_ref, b_ref, y_ref, act_scr, *,
                alpha, n_groups):
    x = x_ref[...]
    logits = jnp.dot(x, wr_ref[...], preferred_element_type=F32,
                     precision=lax.Precision.HIGHEST) + br_ref[...]
    combine = _route(logits, n_groups)
    xb = x.astype(BF16)
    n_experts, _, d_expert = wg_ref.shape
    for e in range(n_experts):
        hg = _dot(xb, wg_ref[e])
        hu = _dot(xb, wu_ref[e])
        act = hg * (1.0 / (1.0 + jnp.exp(-hg))) * hu * combine[:, e:e + 1]
        act_scr[:, e * d_expert:(e + 1) * d_expert] = act.astype(BF16)
    moe = _dot(act_scr[...], wd_ref[...])
    y_ref[...] = _layer_norm(alpha * x + moe, g_ref[...], b_ref[...])


def _moe(x, w_router, b_router, w_gate, w_up, w_down, ln_g, ln_b, *, alpha):
    n, d_model = x.shape
    n_experts, _, d_expert = w_gate.shape
    tm = min(ROW_TILE, n)
    row = lambda i: (i, 0)
    const2 = lambda i: (0, 0)
    const3 = lambda i: (0, 0, 0)
    once = pl.Buffered(1)
    pipelined = 2 * tm * d_model * 4
    resident = 3 * w_gate.size * 2 + tm * n_experts * d_expert * 2 + 2 * d_model * LANES * 4
    temps = 6 * tm * d_model * 4 + 6 * tm * d_expert * 4
    return pl.pallas_call(
        functools.partial(_moe_kernel, alpha=alpha, n_groups=N_GROUPS),
        out_shape=jax.ShapeDtypeStruct((n, d_model), F32),
        grid=(n // tm,),
        in_specs=[pl.BlockSpec((tm, d_model), row),
                  pl.BlockSpec(w_router.shape, const2), pl.BlockSpec(b_router.shape, const2),
                  pl.BlockSpec(w_gate.shape, const3, pipeline_mode=once),
                  pl.BlockSpec(w_up.shape, const3, pipeline_mode=once),
                  pl.BlockSpec(w_down.shape, const2, pipeline_mode=once),
                  pl.BlockSpec(ln_g.shape, const2), pl.BlockSpec(ln_b.shape, const2)],
        out_specs=pl.BlockSpec((tm, d_model), row),
        scratch_shapes=[pltpu.VMEM((tm, n_experts * d_expert), BF16)],
        compiler_params=pltpu.CompilerParams(
            dimension_semantics=("parallel",),
            vmem_limit_bytes=_vmem_limit(pipelined, resident, temps)),
        name="moe",
    )(x, w_router, b_router, w_gate, w_up, w_down, ln_g, ln_b)


def kernel(x_prompt, x_sample, cache_fox_k, cache_fox_v, cache_fox_logf, cache_sb_k, cache_sb_v, page_table, w_in, b_f, g_fox, g_sb, w_o, ln1_g, ln1_b, w_router, b_router, w_gate, w_up, w_down, ln2_g, ln2_b):
    batch, seq_len, d_model = x_prompt.shape
    dec_batch, dec_seq, _ = x_sample.shape
    depth, n_pool, page, h_fox, _ = cache_fox_k.shape
    h_sb = cache_sb_k.shape[3]
    assert h_fox == h_sb and dec_seq == SUBLANES and page == LANES
    d_mix = h_fox * HEAD_DIM
    n_experts, d_expert = w_gate.shape[1], w_gate.shape[3]
    alpha = (2.0 * depth) ** 0.25

    f0 = 3 * d_mix
    w_qkv = jnp.concatenate([w_in[:, :, :f0], w_in[:, :, f0 + h_fox:]], axis=-1).astype(BF16)
    w_f = jnp.pad(w_in[:, :, f0:f0 + h_fox], ((0, 0), (0, 0), (0, LANES - h_fox))).astype(BF16)
    b_f_pad = jnp.pad(b_f, ((0, 0), (0, LANES - h_fox)))[:, None, :]
    w_o_b = w_o.astype(BF16)
    w_gate_b = w_gate.astype(BF16)
    w_up_b = w_up.astype(BF16)
    w_down_b = w_down.reshape(depth, n_experts * d_expert, d_model).astype(BF16)
    row2 = lambda a: a[:, None, :]
    g_fox2, g_sb2, ln1_g2, ln1_b2, ln2_g2, ln2_b2 = map(row2, (g_fox, g_sb, ln1_g, ln1_b, ln2_g, ln2_b))
    b_router2 = b_router[None, :]

    def feature_major(c):
        return jnp.transpose(c, (0, 1, 3, 4, 2)).reshape(depth, n_pool, d_mix, page)

    fox_kt, fox_vt = feature_major(cache_fox_k), feature_major(cache_fox_v)
    sb_kt, sb_vt = feature_major(cache_sb_k), feature_major(cache_sb_v)
    fox_lft = jnp.transpose(cache_fox_logf, (0, 1, 3, 2))

    xp = x_prompt.reshape(batch * seq_len, d_model)
    xs = x_sample.reshape(dec_batch * dec_seq, d_model)
    stacked_p = None
    rows_s = [[] for _ in range(5)]
    for l in range(depth):
        qkv, qb, kb, stacked_p = _in_projection(
            xp, w_qkv[l], w_f[l], b_f_pad[l], seq_len=seq_len, layer=l, depth=depth, stacked=stacked_p)
        o_f = _fox_prompt(qkv, qb, kb, batch=batch, seq_len=seq_len)
        o_s = _sb_prompt(qkv, batch=batch, seq_len=seq_len)
        xp = _out_projection(o_f, o_s, xp, g_fox2[l], g_sb2[l], w_o_b[l], ln1_g2[l], ln1_b2[l], alpha=alpha)
        xp = _moe(xp, w_router, b_router2, w_gate_b[l], w_up_b[l], w_down_b[l], ln2_g2[l], ln2_b2[l], alpha=alpha)

        q32, kf, vf, ks, vs, logf = _in_projection(
            xs, w_qkv[l], w_f[l], b_f_pad[l], seq_len=dec_seq)
        q32 = q32.reshape(dec_batch, dec_seq, 2 * d_mix)
        per_row = lambda a: a.reshape(dec_batch, dec_seq, d_mix)
        lf_new = jnp.transpose(logf.reshape(dec_batch, dec_seq, h_fox), (0, 2, 1))
        lf_new = jnp.pad(lf_new, ((0, 0), (0, 0), (0, page - dec_seq)))
        o_f = _sample_attention("fox", l, page_table, q32, per_row(kf), per_row(vf), fox_kt, fox_vt,
                                lf_new, fox_lft)
        o_s = _sample_attention("sb", l, page_table, q32, per_row(ks), per_row(vs), sb_kt, sb_vt)
        o_f = o_f.reshape(dec_batch * dec_seq, d_mix)
        o_s = o_s.reshape(dec_batch * dec_seq, d_mix)
        xs = _out_projection(o_f, o_s, xs, g_fox2[l], g_sb2[l], w_o_b[l], ln1_g2[l], ln1_b2[l], alpha=alpha)
        xs = _moe(xs, w_router, b_router2, w_gate_b[l], w_up_b[l], w_down_b[l], ln2_g2[l], ln2_b2[l], alpha=alpha)
        for i, r in enumerate((kf, vf, logf, ks, vs)):
            rows_s[i].append(r)

    def stacked(rows, lead, heads):
        a = jnp.stack(rows)
        tail = (heads, HEAD_DIM) if a.shape[-1] != heads else (heads,)
        return a.reshape((depth,) + lead + tail)

    def token_major(a, heads):
        return jnp.transpose(a.reshape(depth, batch, heads, HEAD_DIM, seq_len), (0, 1, 4, 2, 3))

    kf_p, vf_p, ks_p, vs_p, logf_p = stacked_p
    lead_s = (dec_batch, dec_seq)
    return (xp.reshape(batch, seq_len, d_model), xs.reshape(dec_batch, dec_seq, d_model),
            token_major(kf_p, h_fox), token_major(vf_p, h_fox),
            jnp.transpose(logf_p, (0, 1, 3, 2)), token_major(ks_p, h_sb), token_major(vs_p, h_sb),
            stacked(rows_s[0], lead_s, h_fox), stacked(rows_s[1], lead_s, h_fox),
            stacked(rows_s[2], lead_s, h_fox), stacked(rows_s[3], lead_s, h_sb),
            stacked(rows_s[4], lead_s, h_sb))
```
